```python
import jax, jax.numpy as jnp
from jax import lax
import numpy as np

D_MODEL = 2048
BATCH = 4
SEQ = 4096
DEPTH = 4

CHUNK = 64
Q_BLOCK = 128
N_MIXERS = 3
ROPE_THETA = 10000.0
NORM_EPS = 1e-6
D_FF = -(-(8 * D_MODEL) // (3 * 256)) * 256

DSA_HEADS = 16
DSA_HEAD_DIM = D_MODEL // DSA_HEADS
DSA_KV_HEADS = 4
DSA_GROUP = DSA_HEADS // DSA_KV_HEADS
IDX_HEADS = 16
IDX_DIM = 128
IDX_ROPE_DIM = 64
TOPK_MAX = 256

MLA_HEADS = 16
MLA_Q_RANK = 512
MLA_KV_RANK = 512
MLA_NOPE_DIM = 128
MLA_ROPE_DIM = 64
MLA_V_DIM = 128

SB_HEADS = 16
SB_HEAD_DIM = D_MODEL // SB_HEADS

kernel_name = 'hybrid_dsa_mla_stickbreaking_trunk'

F32 = jnp.float32


def rms_norm(x, g):
    xf = x.astype(F32)
    y = xf * lax.rsqrt(jnp.mean(xf * xf, axis=-1, keepdims=True) + NORM_EPS)
    return (y * g.astype(F32)).astype(x.dtype)


def rope(x, pos):
    d = x.shape[-1]
    inv_freq = ROPE_THETA ** (-jnp.arange(0, d, 2, dtype=F32) / d)
    ang = pos.astype(F32)[:, None] * inv_freq[None, :]
    ang = ang.reshape(ang.shape[0], *([1] * (x.ndim - 3)), d // 2)
    cos, sin = jnp.cos(ang), jnp.sin(ang)
    xf = x.astype(F32)
    x1, x2 = xf[..., : d // 2], xf[..., d // 2:]
    return jnp.concatenate([x1 * cos - x2 * sin, x2 * cos + x1 * sin], axis=-1).astype(x.dtype)


def _split_cols(a, sizes):
    points, acc = [], 0
    for s in sizes[:-1]:
        acc += s
        points.append(acc)
    return jnp.split(a, points, axis=-1)


def _sweep_query_blocks(fn, arrays):
    seq = arrays[0].shape[1]
    n_blocks = seq // Q_BLOCK

    def split(a):
        return jnp.moveaxis(a.reshape(a.shape[0], n_blocks, Q_BLOCK, *a.shape[2:]), 1, 0)

    starts = jnp.arange(n_blocks, dtype=jnp.int32) * Q_BLOCK
    out = lax.map(lambda xs: fn(xs[0], *xs[1]), (starts, tuple(split(a) for a in arrays)))
    out = jnp.moveaxis(out, 0, 1)
    return out.reshape(out.shape[0], seq, *out.shape[3:])


def dsa_mixer(h, w_in, w_out, pos):
    b, s, _ = h.shape
    top_k = min(TOPK_MAX, s // 4)
    q, k, v, iq, ik, iw = _split_cols(h @ w_in, [
        DSA_HEADS * DSA_HEAD_DIM, DSA_KV_HEADS * DSA_HEAD_DIM, DSA_KV_HEADS * DSA_HEAD_DIM,
        IDX_HEADS * IDX_DIM, IDX_DIM, IDX_HEADS])
    q = rope(q.reshape(b, s, DSA_KV_HEADS, DSA_GROUP, DSA_HEAD_DIM), pos)
    k = rope(k.reshape(b, s, DSA_KV_HEADS, DSA_HEAD_DIM), pos)
    v = v.reshape(b, s, DSA_KV_HEADS, DSA_HEAD_DIM)
    iq = iq.reshape(b, s, IDX_HEADS, IDX_DIM)
    iq = jnp.concatenate([rope(iq[..., :IDX_ROPE_DIM], pos), iq[..., IDX_ROPE_DIM:]], axis=-1)
    ik = jnp.concatenate([rope(ik[..., :IDX_ROPE_DIM], pos), ik[..., IDX_ROPE_DIM:]], axis=-1)
    key_chunk = pos // CHUNK
    gather = jax.vmap(lambda table, idx: table[idx])

    def block(start, q_b, iq_b, iw_b):
        q_chunk = (start + jnp.arange(Q_BLOCK, dtype=jnp.int32)) // CHUNK
        visible = key_chunk[None, :] <= q_chunk[:, None]
        rel = jax.nn.relu(jnp.einsum('bqhd,bsd->bqhs', iq_b, ik).astype(F32) * IDX_DIM ** -0.5)
        score = jnp.einsum('bqhs,bqh->bqs', rel, iw_b.astype(F32) * IDX_HEADS ** -0.5)
        score = jnp.where(visible[None], score, -jnp.inf)
        _, sel = lax.top_k(score, top_k)
        sel_ok = (sel // CHUNK) <= q_chunk[None, :, None]
        k_sel = gather(k, sel)
        v_sel = gather(v, sel)
        logits = jnp.einsum('bqgrd,bqkgd->bqgrk', q_b, k_sel).astype(F32) * DSA_HEAD_DIM ** -0.5
        logits = jnp.where(sel_ok[:, :, None, None, :], logits, -jnp.inf)
        p = jax.nn.softmax(logits, axis=-1).astype(v.dtype)
        o = jnp.einsum('bqgrk,bqkgd->bqgrd', p, v_sel)
        return o.reshape(o.shape[0], Q_BLOCK, DSA_HEADS * DSA_HEAD_DIM)

    o = _sweep_query_blocks(block, (q, iq, iw))
    return o @ w_out


def mla_mixer(h, w_down, q_norm, kv_norm, w_uq, w_ukv, w_out, pos):
    b, s, _ = h.shape
    cq, ckv, k_rope = _split_cols(h @ w_down, [MLA_Q_RANK, MLA_KV_RANK, MLA_ROPE_DIM])
    cq = rms_norm(cq, q_norm)
    ckv = rms_norm(ckv, kv_norm)
    q = (cq @ w_uq).reshape(b, s, MLA_HEADS, MLA_NOPE_DIM + MLA_ROPE_DIM)
    q_nope, q_rope = q[..., :MLA_NOPE_DIM], rope(q[..., MLA_NOPE_DIM:], pos)
    kv = (ckv @ w_ukv).reshape(b, s, MLA_HEADS, MLA_NOPE_DIM + MLA_V_DIM)
    k_nope, v = kv[..., :MLA_NOPE_DIM], kv[..., MLA_NOPE_DIM:]
    k_rope = rope(k_rope, pos)
    scale = (MLA_NOPE_DIM + MLA_ROPE_DIM) ** -0.5
    key_chunk = pos // CHUNK

    def block(start, qn_b, qr_b):
        q_chunk = (start + jnp.arange(Q_BLOCK, dtype=jnp.int32)) // CHUNK
        visible = key_chunk[None, :] <= q_chunk[:, None]
        logits = (jnp.einsum('bqhd,bshd->bhqs', qn_b, k_nope)
                  + jnp.einsum('bqhd,bsd->bhqs', qr_b, k_rope)).astype(F32) * scale
        logits = jnp.where(visible[None, None], logits, -jnp.inf)
        p = jax.nn.softmax(logits, axis=-1).astype(v.dtype)
        o = jnp.einsum('bhqs,bshd->bqhd', p, v)
        return o.reshape(o.shape[0], Q_BLOCK, MLA_HEADS * MLA_V_DIM)

    o = _sweep_query_blocks(block, (q_nope, q_rope))
    return o @ w_out


def stick_breaking_mixer(h, w_in, w_out, pos):
    b, s, _ = h.shape
    q, k, v = [a.reshape(b, s, SB_HEADS, SB_HEAD_DIM) for a in jnp.split(h @ w_in, 3, axis=-1)]

    def block(start, q_b):
        q_pos = start + jnp.arange(Q_BLOCK, dtype=jnp.int32)
        before = pos[None, :] < q_pos[:, None]
        z = jnp.einsum('bqhd,bshd->bhqs', q_b, k).astype(F32) * SB_HEAD_DIM ** -0.5
        log_beta = jax.nn.log_sigmoid(z)
        log_keep = jnp.where(before, jax.nn.log_sigmoid(-z), 0.0)
        later_keep = lax.cumsum(log_keep, axis=3, reverse=True) - log_keep
        a = jnp.where(before, jnp.exp(log_beta + later_keep), 0.0).astype(v.dtype)
        o = jnp.einsum('bhqs,bshd->bqhd', a, v)
        return o.reshape(o.shape[0], Q_BLOCK, SB_HEADS * SB_HEAD_DIM)

    o = _sweep_query_blocks(block, (q,))
    return o @ w_out


def swiglu(h, w_in, w_out):
    gate, up = jnp.split(h @ w_in, 2, axis=-1)
    return (jax.nn.silu(gate) * up) @ w_out


def adaln(c, w, b):
    mod = jax.nn.silu(c) @ w + b
    return jnp.split(mod[:, None, :], 6, axis=-1)


def setup_inputs(seed: int = 0) -> dict:
    key = jax.random.key(seed)
    keys = iter(jax.random.split(key, 80))

    def dense(fan_in, fan_out):
        return jax.random.normal(next(keys), (fan_in, fan_out), F32) * fan_in ** -0.5

    def gain(n):
        return 1.0 + 0.05 * jax.random.normal(next(keys), (n,), F32)

    def bias(n):
        return 0.02 * jax.random.normal(next(keys), (n,), F32)

    inputs = {}
    inputs['x'] = jax.random.normal(next(keys), (BATCH, SEQ, D_MODEL), F32)
    inputs['c'] = jax.random.normal(next(keys), (BATCH, D_MODEL), F32)
    dsa_in = (DSA_HEADS + 2 * DSA_KV_HEADS) * DSA_HEAD_DIM + IDX_HEADS * IDX_DIM + IDX_DIM + IDX_HEADS
    for i in range(DEPTH):
        inputs[f'ada_w_{i}'] = dense(D_MODEL, 6 * D_MODEL)
        inputs[f'ada_b_{i}'] = bias(6 * D_MODEL)
        inputs[f'norm_mix_{i}'] = gain(D_MODEL)
        inputs[f'norm_ffn_{i}'] = gain(D_MODEL)
        kind = i % N_MIXERS
        if kind == 0:
            inputs[f'dsa_w_in_{i}'] = dense(D_MODEL, dsa_in)
            inputs[f'dsa_w_out_{i}'] = dense(DSA_HEADS * DSA_HEAD_DIM, D_MODEL)
        elif kind == 1:
            inputs[f'mla_w_down_{i}'] = dense(D_MODEL, MLA_Q_RANK + MLA_KV_RANK + MLA_ROPE_DIM)
            inputs[f'mla_q_norm_{i}'] = gain(MLA_Q_RANK)
            inputs[f'mla_kv_norm_{i}'] = gain(MLA_KV_RANK)
            inputs[f'mla_w_uq_{i}'] = dense(MLA_Q_RANK, MLA_HEADS * (MLA_NOPE_DIM + MLA_ROPE_DIM))
            inputs[f'mla_w_ukv_{i}'] = dense(MLA_KV_RANK, MLA_HEADS * (MLA_NOPE_DIM + MLA_V_DIM))
            inputs[f'mla_w_out_{i}'] = dense(MLA_HEADS * MLA_V_DIM, D_MODEL)
        else:
            inputs[f'sb_w_in_{i}'] = dense(D_MODEL, 3 * SB_HEADS * SB_HEAD_DIM)
            inputs[f'sb_w_out_{i}'] = dense(SB_HEADS * SB_HEAD_DIM, D_MODEL)
        inputs[f'ffn_w_in_{i}'] = dense(D_MODEL, 2 * D_FF)
        inputs[f'ffn_w_out_{i}'] = dense(D_FF, D_MODEL)
    inputs['final_norm'] = gain(D_MODEL)
    return inputs


def reference(x, c,
              ada_w_0, ada_b_0, norm_mix_0, norm_ffn_0, dsa_w_in_0, dsa_w_out_0, ffn_w_in_0, ffn_w_out_0,
              ada_w_1, ada_b_1, norm_mix_1, norm_ffn_1, mla_w_down_1, mla_q_norm_1, mla_kv_norm_1,
              mla_w_uq_1, mla_w_ukv_1, mla_w_out_1, ffn_w_in_1, ffn_w_out_1,
              ada_w_2, ada_b_2, norm_mix_2, norm_ffn_2, sb_w_in_2, sb_w_out_2, ffn_w_in_2, ffn_w_out_2,
              ada_w_3, ada_b_3, norm_mix_3, norm_ffn_3, dsa_w_in_3, dsa_w_out_3, ffn_w_in_3, ffn_w_out_3,
              final_norm):
    pos = jnp.arange(x.shape[1], dtype=jnp.int32)
    mixers = (dsa_mixer, mla_mixer, stick_breaking_mixer)
    layers = (
        (ada_w_0, ada_b_0, norm_mix_0, norm_ffn_0, (dsa_w_in_0, dsa_w_out_0), ffn_w_in_0, ffn_w_out_0),
        (ada_w_1, ada_b_1, norm_mix_1, norm_ffn_1,
         (mla_w_down_1, mla_q_norm_1, mla_kv_norm_1, mla_w_uq_1, mla_w_ukv_1, mla_w_out_1),
         ffn_w_in_1, ffn_w_out_1),
        (ada_w_2, ada_b_2, norm_mix_2, norm_ffn_2, (sb_w_in_2, sb_w_out_2), ffn_w_in_2, ffn_w_out_2),
        (ada_w_3, ada_b_3, norm_mix_3, norm_ffn_3, (dsa_w_in_3, dsa_w_out_3), ffn_w_in_3, ffn_w_out_3),
    )
    for i in range(DEPTH):
        ada_w, ada_b, g_mix, g_ffn, mix_params, w_ff_in, w_ff_out = layers[i]
        sh1, sc1, gt1, sh2, sc2, gt2 = adaln(c, ada_w, ada_b)
        h = rms_norm(x, g_mix) * (1 + sc1) + sh1
        x = x + gt1 * mixers[i % N_MIXERS](h, *mix_params, pos)
        h = rms_norm(x, g_ffn) * (1 + sc2) + sh2
        x = x + gt2 * swiglu(h, w_ff_in, w_ff_out)
    return rms_norm(x, final_norm)
```

```python
import functools
import math

import numpy as np
import jax
import jax.numpy as jnp
from jax import lax
from jax.experimental import pallas as pl
from jax.experimental.pallas import tpu as pltpu

F32 = jnp.float32
BF16 = jnp.bfloat16
I32 = jnp.int32

V7X_VMEM_BYTES = 64 * 1024 * 1024
VMEM_LIMIT_BYTES = V7X_VMEM_BYTES * 7 // 8
LANES = 128
SUBLANES = 8

CHUNK = 64
CHUNK_SHIFT = int(math.log2(CHUNK))
assert 1 << CHUNK_SHIFT == CHUNK
ROPE_THETA = 10000.0
NORM_EPS = 1e-6
DSA_HEADS = 16
DSA_KV_HEADS = 4
DSA_GROUP = DSA_HEADS // DSA_KV_HEADS
HEAD_DIM = 128
IDX_HEADS = 16
IDX_DIM = 128
IDX_ROPE_DIM = 64
TOPK_MAX = 256
MLA_HEADS = 16
MLA_Q_RANK = 512
MLA_KV_RANK = 512
MLA_NOPE_DIM = 128
MLA_ROPE_DIM = 64
MLA_V_DIM = 128
SB_HEADS = 16

NEG_BIG = -1e30
INT32_MIN = -(2 ** 31)
NEG_INF_KEY = int(np.int32(np.uint32(0xFF800000) ^ np.uint32(0x7FFFFFFF)))

_NT = (((1,), (1,)), ((), ()))


def _params(n_grid_dims):
    return pltpu.CompilerParams(
        dimension_semantics=("arbitrary",) * n_grid_dims,
        vmem_limit_bytes=VMEM_LIMIT_BYTES)


def _sigmoid(x):
    return 1.0 / (1.0 + jnp.exp(-x))


def _adaln_kernel(c_ref, w_ref, b_ref, o_ref):
    c = c_ref[...]
    s = (c * _sigmoid(c)).astype(BF16)
    o_ref[...] = jnp.dot(s, w_ref[...].astype(BF16), preferred_element_type=F32) + b_ref[...]


def _adaln(c_pad, w, b):
    rows, d = c_pad.shape
    n = w.shape[1]
    tn = 1024
    return pl.pallas_call(
        _adaln_kernel,
        grid=(n // tn,),
        in_specs=[pl.BlockSpec((rows, d), lambda j: (0, 0)),
                  pl.BlockSpec((d, tn), lambda j: (0, j)),
                  pl.BlockSpec((1, tn), lambda j: (0, j))],
        out_specs=pl.BlockSpec((rows, tn), lambda j: (0, j)),
        out_shape=jax.ShapeDtypeStruct((rows, n), F32),
        compiler_params=_params(1),
        name="adaln",
    )(c_pad, w, b.reshape(1, n))


def _rms(x, g):
    return x * lax.rsqrt(jnp.mean(x * x, axis=-1, keepdims=True) + NORM_EPS) * g


def _norm_mod_kernel(x_ref, g_ref, sc_ref, sh_ref, o_ref):
    y = _rms(x_ref[...], g_ref[...])
    o_ref[...] = (y * (1.0 + sc_ref[...]) + sh_ref[...]).astype(o_ref.dtype)


def _norm_kernel(x_ref, g_ref, o_ref):
    o_ref[...] = _rms(x_ref[...], g_ref[...]).astype(o_ref.dtype)


def _norm_mod(x2, g, mod3, sc_idx, sh_idx, seq):
    m, d = x2.shape
    ts = 512
    per = seq // ts
    return pl.pallas_call(
        _norm_mod_kernel,
        grid=(m // ts,),
        in_specs=[pl.BlockSpec((ts, d), lambda i: (i, 0)),
                  pl.BlockSpec((1, d), lambda i: (0, 0)),
                  pl.BlockSpec((None, 1, d), lambda i: ((i // per) * 6 + sc_idx, 0, 0)),
                  pl.BlockSpec((None, 1, d), lambda i: ((i // per) * 6 + sh_idx, 0, 0))],
        out_specs=pl.BlockSpec((ts, d), lambda i: (i, 0)),
        out_shape=jax.ShapeDtypeStruct((m, d), BF16),
        compiler_params=_params(1),
        name="norm_mod",
    )(x2, g.reshape(1, d), mod3, mod3)


def _norm_final(x2, g):
    m, d = x2.shape
    ts = 512
    return pl.pallas_call(
        _norm_kernel,
        grid=(m // ts,),
        in_specs=[pl.BlockSpec((ts, d), lambda i: (i, 0)),
                  pl.BlockSpec((1, d), lambda i: (0, 0))],
        out_specs=pl.BlockSpec((ts, d), lambda i: (i, 0)),
        out_shape=jax.ShapeDtypeStruct((m, d), F32),
        compiler_params=_params(1),
        name="norm_final",
    )(x2, g.reshape(1, d))


def _dot(a_ref, w_ref):
    return jnp.dot(a_ref[...], w_ref[...], preferred_element_type=F32)


def _mm_plain_kernel(a_ref, w_ref, o_ref):
    o_ref[...] = _dot(a_ref, w_ref).astype(o_ref.dtype)


def _mm_rope_kernel(a_ref, w_ref, t0_ref, t1_ref, t2_ref, o_ref, *, kinds, shift):
    acc = _dot(a_ref, w_ref)
    period = len(kinds)
    for c in range(acc.shape[1] // LANES):
        y = acc[:, c * LANES:(c + 1) * LANES]
        k = c % period
        if kinds[k]:
            t = slice(k * LANES, (k + 1) * LANES)
            out = y * t0_ref[:, t] + pltpu.roll(y, LANES - shift, 1) * t1_ref[:, t]
            if 2 * shift != LANES:
                out = out + pltpu.roll(y, shift, 1) * t2_ref[:, t]
        else:
            out = y
        o_ref[:, c * LANES:(c + 1) * LANES] = out.astype(o_ref.dtype)


def _mm_rms_kernel(a_ref, w_ref, g_ref, o_ref):
    o_ref[...] = _rms(_dot(a_ref, w_ref), g_ref[...]).astype(o_ref.dtype)


def _mm_swiglu_kernel(a_ref, wg_ref, wu_ref, o_ref):
    a = a_ref[...]
    g = jnp.dot(a, wg_ref[...], preferred_element_type=F32)
    u = jnp.dot(a, wu_ref[...], preferred_element_type=F32)
    o_ref[...] = (g * _sigmoid(g) * u).astype(o_ref.dtype)


def _mm_resid_kernel(a_ref, w_ref, x_ref, gt_ref, o_ref):
    o_ref[...] = x_ref[...] + gt_ref[...] * _dot(a_ref, w_ref)


def _mm_call(kernel, a, weights, w_offs, extras, extra_specs, n_out, out_dtype, tm, tn, name):
    m, k = a.shape
    assert m % tm == 0 and n_out % tn == 0
    in_specs = [pl.BlockSpec((tm, k), lambda j, i: (i, 0))]
    for off in w_offs:
        assert off % tn == 0
        in_specs.append(pl.BlockSpec((k, tn), lambda j, i, o=off // tn: (0, j + o)))
    in_specs += extra_specs
    return pl.pallas_call(
        kernel,
        grid=(n_out // tn, m // tm),
        in_specs=in_specs,
        out_specs=pl.BlockSpec((tm, tn), lambda j, i: (i, j)),
        out_shape=jax.ShapeDtypeStruct((m, n_out), out_dtype),
        compiler_params=_params(2),
        name=name,
    )(a, *weights, *extras)


def _mm_plain(a, w, col_off, n_out, out_dtype=BF16, tm=1024, tn=512):
    tn = min(tn, n_out)
    return _mm_call(_mm_plain_kernel, a, [w], [col_off], [], [], n_out, out_dtype, tm, tn, "mm_plain")


def _mm_rope(a, w, col_off, n_out, tables, kinds, shift, seq, tm=1024, tn=512):
    tn = min(tn, n_out)
    per = seq // tm
    width = tables[0].shape[1]
    assert width == LANES * len(kinds) and tn % width == 0
    specs = [pl.BlockSpec((tm, width), lambda j, i: (i % per, 0))] * 3
    kern = functools.partial(_mm_rope_kernel, kinds=kinds, shift=shift)
    return _mm_call(kern, a, [w], [col_off], list(tables), specs, n_out, BF16, tm, tn, "mm_rope")


def _mm_rms(a, w, col_off, g, tm=1024):
    n = g.shape[0]
    specs = [pl.BlockSpec((1, n), lambda j, i: (0, 0))]
    return _mm_call(_mm_rms_kernel, a, [w], [col_off], [g.reshape(1, n)], specs, n, BF16, tm, n, "mm_rms")


def _mm_swiglu(a, w, d_ff, tm=1024, tn=512):
    return _mm_call(_mm_swiglu_kernel, a, [w, w], [0, d_ff], [], [], d_ff, BF16, tm, tn, "mm_swiglu")


def _mm_resid(a, w, x2, mod3, gate_idx, seq, tm=512, tn=512):
    per = seq // tm
    specs = [pl.BlockSpec((tm, tn), lambda j, i: (i, j)),
             pl.BlockSpec((None, 1, tn), lambda j, i: ((i // per) * 6 + gate_idx, 0, j))]
    return _mm_call(_mm_resid_kernel, a, [w], [0], [x2, mod3], specs, w.shape[1], F32, tm, tn, "mm_resid")


def _rope_tables(seq, rot_dim, pad_to):
    half = rot_dim // 2
    inv_freq = ROPE_THETA ** (-jnp.arange(0, rot_dim, 2, dtype=F32) / rot_dim)
    ang = jnp.arange(seq, dtype=F32)[:, None] * inv_freq[None, :]
    cos, sin = jnp.cos(ang), jnp.sin(ang)
    zero = jnp.zeros_like(sin)
    tail1 = jnp.ones((seq, pad_to - rot_dim), F32)
    tail0 = jnp.zeros((seq, pad_to - rot_dim), F32)
    if 2 * half == pad_to == LANES:
        return (jnp.concatenate([cos, cos], 1), jnp.concatenate([-sin, sin], 1),
                jnp.zeros((seq, pad_to), F32))
    t0 = jnp.concatenate([cos, cos, tail1], 1)
    t1 = jnp.concatenate([-sin, zero, tail0], 1)
    t2 = jnp.concatenate([zero, sin, tail0], 1)
    return t0, t1, t2


DSA_QB = 128
DSA_KB = 512


def _sortable_key(x):
    b = lax.bitcast_convert_type(x, I32)
    return b ^ ((b >> 31) & 0x7FFFFFFF)


def _dsa_kernel(q_ref, iq_ref, wt_ref, k_ref, v_ref, ik_ref, o_ref,
                keys_ref, bias_ref, acc_ref, m_ref, l_ref, *, top_k):
    qb, kb_sz = DSA_QB, DSA_KB
    start = pl.program_id(1) * qb
    n_blk = (start + qb + kb_sz - 1) // kb_sz

    w_scaled = wt_ref[...] * (IDX_DIM ** -0.5 * IDX_HEADS ** -0.5)
    q_chunk = (start + lax.broadcasted_iota(I32, (kb_sz, qb), 1)) >> CHUNK_SHIFT
    k_iota = lax.broadcasted_iota(I32, (kb_sz, qb), 0)

    def score_blk(kb, carry):
        koff = pl.multiple_of(kb * kb_sz, kb_sz)
        ikb = ik_ref[pl.ds(koff, kb_sz), :]
        acc = jnp.zeros((kb_sz, qb), F32)
        for h in range(IDX_HEADS):
            r = lax.dot_general(ikb, iq_ref[:, h * IDX_DIM:(h + 1) * IDX_DIM], _NT,
                                preferred_element_type=F32)
            acc = acc + jnp.maximum(r, 0.0) * w_scaled[h:h + 1, :]
        visible = ((koff + k_iota) >> CHUNK_SHIFT) <= q_chunk
        keys_ref[pl.ds(koff, kb_sz), :] = _sortable_key(jnp.where(visible, acc, -jnp.inf))
        return carry

    lax.fori_loop(0, n_blk, score_blk, 0)

    def bit_step(b, thr):
        cand = thr + lax.shift_left(jnp.int32(1), 31 - b)

        def count_blk(kb, cnt):
            koff = pl.multiple_of(kb * kb_sz, kb_sz)
            ge = (keys_ref[pl.ds(koff, kb_sz), :] >= cand).astype(I32)
            return cnt + ge.reshape(kb_sz // SUBLANES, SUBLANES, qb).sum(axis=0)

        cnt = lax.fori_loop(0, n_blk, count_blk, jnp.zeros((SUBLANES, qb), I32))
        total = cnt.sum(axis=0, keepdims=True)
        return jnp.where(total >= top_k, cand, thr)

    thr = lax.fori_loop(0, 32, bit_step, jnp.full((1, qb), INT32_MIN, I32))

    def bias_blk(kb, carry):
        koff = pl.multiple_of(kb * kb_sz, kb_sz)
        blk = keys_ref[pl.ds(koff, kb_sz), :]
        bias_t = jnp.where(blk >= thr, jnp.where(blk > NEG_INF_KEY, 0.0, NEG_BIG), NEG_BIG)
        bias_ref[kb] = bias_t.astype(F32).T
        return carry

    lax.fori_loop(0, n_blk, bias_blk, 0)

    scale = HEAD_DIM ** -0.5
    for g in range(DSA_KV_HEADS):
        heads = [g * DSA_GROUP + r for r in range(DSA_GROUP)]
        qg = jnp.concatenate([q_ref[:, h * HEAD_DIM:(h + 1) * HEAD_DIM] for h in heads], axis=0)
        m_ref[...] = jnp.full(m_ref.shape, NEG_BIG, F32)
        l_ref[...] = jnp.zeros(l_ref.shape, F32)
        acc_ref[...] = jnp.zeros(acc_ref.shape, F32)

        def att_blk(kb, carry, g=g, qg=qg):
            koff = pl.multiple_of(kb * kb_sz, kb_sz)
            kblk = k_ref[pl.ds(koff, kb_sz), g * HEAD_DIM:(g + 1) * HEAD_DIM]
            vblk = v_ref[pl.ds(koff, kb_sz), g * HEAD_DIM:(g + 1) * HEAD_DIM]
            s = lax.dot_general(qg, kblk, _NT, preferred_element_type=F32) * scale
            s = (s.reshape(DSA_GROUP, qb, kb_sz) + bias_ref[kb][None]).reshape(DSA_GROUP * qb, kb_sz)
            m_prev = m_ref[...]
            m_new = jnp.maximum(m_prev, s.max(axis=1, keepdims=True))
            alpha = jnp.exp(m_prev - m_new)
            p = jnp.exp(s - m_new)
            l_ref[...] = alpha * l_ref[...] + p.sum(axis=1, keepdims=True)
            acc_ref[...] = alpha * acc_ref[...] + jnp.dot(p.astype(BF16), vblk,
                                                          preferred_element_type=F32)
            m_ref[...] = m_new
            return carry

        lax.fori_loop(0, n_blk, att_blk, 0)
        o = acc_ref[...] / l_ref[...]
        for r, h in enumerate(heads):
            o_ref[:, h * HEAD_DIM:(h + 1) * HEAD_DIM] = o[r * qb:(r + 1) * qb].astype(o_ref.dtype)


def _dsa_attention(qk, v, iq, ik, iw_t, batch, seq):
    m = qk.shape[0]
    qb, kb_sz = DSA_QB, DSA_KB
    assert seq % kb_sz == 0
    nq = seq // qb
    top_k = min(TOPK_MAX, seq // 4)
    d_q = DSA_HEADS * HEAD_DIM
    d_kv = DSA_KV_HEADS * HEAD_DIM
    return pl.pallas_call(
        functools.partial(_dsa_kernel, top_k=top_k),
        grid=(batch, nq),
        in_specs=[pl.BlockSpec((qb, d_q), lambda b, j: (b * nq + j, 0)),
                  pl.BlockSpec((qb, IDX_HEADS * IDX_DIM), lambda b, j: (b * nq + j, 0)),
                  pl.BlockSpec((IDX_HEADS, qb), lambda b, j: (b, j)),
                  pl.BlockSpec((seq, d_kv), lambda b, j: (b, d_q // d_kv)),
                  pl.BlockSpec((seq, d_kv), lambda b, j: (b, 0)),
                  pl.BlockSpec((seq, IDX_DIM), lambda b, j: (b, 0))],
        out_specs=pl.BlockSpec((qb, d_q), lambda b, j: (b * nq + j, 0)),
        out_shape=jax.ShapeDtypeStruct((m, d_q), BF16),
        scratch_shapes=[pltpu.VMEM((seq, qb), I32),
                        pltpu.VMEM((seq // kb_sz, qb, kb_sz), F32),
                        pltpu.VMEM((DSA_GROUP * qb, HEAD_DIM), F32),
                        pltpu.VMEM((DSA_GROUP * qb, 1), F32),
                        pltpu.VMEM((DSA_GROUP * qb, 1), F32)],
        compiler_params=_params(2),
        name="dsa_attention",
    )(qk, iq, iw_t, qk, v, ik)


def _dsa_mixer(h, w_in, batch, seq):
    d_q = DSA_HEADS * HEAD_DIM
    d_kv = DSA_KV_HEADS * HEAD_DIM
    d_iq = IDX_HEADS * IDX_DIM
    off_v = d_q + d_kv
    off_iq = off_v + d_kv
    off_ik = off_iq + d_iq
    off_iw = off_ik + IDX_DIM
    w = w_in.astype(BF16)
    w_iw = jnp.pad(w[:, off_iw:], ((0, 0), (0, LANES - IDX_HEADS)))
    full_tabs = _rope_tables(seq, HEAD_DIM, HEAD_DIM)
    idx_tabs = _rope_tables(seq, IDX_ROPE_DIM, IDX_DIM)
    qk = _mm_rope(h, w, 0, d_q + d_kv, full_tabs, (True,), HEAD_DIM // 2, seq)
    v = _mm_plain(h, w, off_v, d_kv)
    iq = _mm_rope(h, w, off_iq, d_iq, idx_tabs, (True,), IDX_ROPE_DIM // 2, seq)
    ik = _mm_rope(h, w, off_ik, IDX_DIM, idx_tabs, (True,), IDX_ROPE_DIM // 2, seq)
    iw = _mm_plain(h, w_iw, 0, LANES, out_dtype=F32)
    iw_t = iw[:, :IDX_HEADS].reshape(batch, seq, IDX_HEADS).transpose(0, 2, 1)
    iw_t = iw_t.reshape(batch * IDX_HEADS, seq)
    return _dsa_attention(qk, v, iq, ik, iw_t, batch, seq)


MLA_TQ = 256
MLA_TK = 512
MLA_QK_PAD = 2 * LANES


def _mla_kernel(q_ref, kn_ref, kr_ref, v_ref, o_ref, acc_ref, m_ref, l_ref):
    tq, tk = MLA_TQ, MLA_TK
    start = pl.program_id(2) * tq
    n_blk = (start + tq + tk - 1) // tk
    q = q_ref[...]
    scale = (MLA_NOPE_DIM + MLA_ROPE_DIM) ** -0.5
    q_chunk = (start + lax.broadcasted_iota(I32, (tq, tk), 0)) >> CHUNK_SHIFT
    k_iota = lax.broadcasted_iota(I32, (tq, tk), 1)
    m_ref[...] = jnp.full(m_ref.shape, NEG_BIG, F32)
    l_ref[...] = jnp.zeros(l_ref.shape, F32)
    acc_ref[...] = jnp.zeros(acc_ref.shape, F32)

    def blk(kb, carry):
        koff = pl.multiple_of(kb * tk, tk)
        kcat = jnp.concatenate([kn_ref[pl.ds(koff, tk), :], kr_ref[pl.ds(koff, tk), :]], axis=1)
        s = lax.dot_general(q, kcat, _NT, preferred_element_type=F32) * scale
        s = jnp.where(((koff + k_iota) >> CHUNK_SHIFT) <= q_chunk, s, NEG_BIG)
        m_prev = m_ref[...]
        m_new = jnp.maximum(m_prev, s.max(axis=1, keepdims=True))
        alpha = jnp.exp(m_prev - m_new)
        p = jnp.exp(s - m_new)
        l_ref[...] = alpha * l_ref[...] + p.sum(axis=1, keepdims=True)
        acc_ref[...] = alpha * acc_ref[...] + jnp.dot(p.astype(BF16), v_ref[pl.ds(koff, tk), :],
                                                      preferred_element_type=F32)
        m_ref[...] = m_new
        return carry

    lax.fori_loop(0, n_blk, blk, 0)
    o_ref[...] = (acc_ref[...] / l_ref[...]).astype(o_ref.dtype)


def _mla_attention(q, kv, k_rope, batch, seq):
    m = q.shape[0]
    tq, tk = MLA_TQ, MLA_TK
    assert seq % tk == 0 and seq % tq == 0
    nq = seq // tq
    return pl.pallas_call(
        _mla_kernel,
        grid=(batch, MLA_HEADS, nq),
        in_specs=[pl.BlockSpec((tq, MLA_QK_PAD), lambda b, h, i: (b * nq + i, h)),
                  pl.BlockSpec((seq, MLA_NOPE_DIM), lambda b, h, i: (b, 2 * h)),
                  pl.BlockSpec((seq, LANES), lambda b, h, i: (b, 0)),
                  pl.BlockSpec((seq, MLA_V_DIM), lambda b, h, i: (b, 2 * h + 1))],
        out_specs=pl.BlockSpec((tq, MLA_V_DIM), lambda b, h, i: (b * nq + i, h)),
        out_shape=jax.ShapeDtypeStruct((m, MLA_HEADS * MLA_V_DIM), BF16),
        scratch_shapes=[pltpu.VMEM((tq, MLA_V_DIM), F32),
                        pltpu.VMEM((tq, 1), F32),
                        pltpu.VMEM((tq, 1), F32)],
        compiler_params=_params(3),
        name="mla_attention",
    )(q, kv, k_rope, kv)


def _mla_mixer(h, w_down, q_norm, kv_norm, w_uq, w_ukv, batch, seq):
    wd = w_down.astype(BF16)
    wd_rope = jnp.pad(wd[:, MLA_Q_RANK + MLA_KV_RANK:], ((0, 0), (0, LANES - MLA_ROPE_DIM)))
    w_uq_h = w_uq.astype(BF16).reshape(MLA_Q_RANK, MLA_HEADS, MLA_NOPE_DIM + MLA_ROPE_DIM)
    pad = MLA_QK_PAD - MLA_NOPE_DIM - MLA_ROPE_DIM
    w_uq_p = jnp.pad(w_uq_h, ((0, 0), (0, 0), (0, pad))).reshape(MLA_Q_RANK, MLA_HEADS * MLA_QK_PAD)
    rope_tabs = _rope_tables(seq, MLA_ROPE_DIM, LANES)
    q_tabs = tuple(jnp.concatenate([jnp.zeros_like(t), t], axis=1) for t in rope_tabs)

    cq = _mm_rms(h, wd, 0, q_norm)
    ckv = _mm_rms(h, wd, MLA_Q_RANK, kv_norm)
    k_rope = _mm_rope(h, wd_rope, 0, LANES, rope_tabs, (True,), MLA_ROPE_DIM // 2, seq)
    q = _mm_rope(cq, w_uq_p, 0, MLA_HEADS * MLA_QK_PAD, q_tabs, (False, True), MLA_ROPE_DIM // 2, seq)
    kv = _mm_plain(ckv, w_ukv.astype(BF16), 0, MLA_HEADS * (MLA_NOPE_DIM + MLA_V_DIM))
    return _mla_attention(q, kv, k_rope, batch, seq)


SB_T = 256


def _sb_kernel(q_ref, k_ref, v_ref, o_ref):
    t = SB_T
    i = pl.program_id(2)
    start = i * t
    q = q_ref[...]
    scale = HEAD_DIM ** -0.5
    row = lax.broadcasted_iota(I32, (t, t), 0)
    col = lax.broadcasted_iota(I32, (t, t), 1)
    later = (row > col).astype(BF16)

    def blk(n, carry):
        acc, c = carry
        koff = pl.multiple_of((i - n) * t, t)
        z = lax.dot_general(q, k_ref[pl.ds(koff, t), :], _NT, preferred_element_type=F32) * scale
        sp = jnp.log1p(jnp.exp(-jnp.abs(z)))
        log_beta = jnp.minimum(z, 0.0) - sp
        before = (koff + col) < (start + row)
        log_keep = jnp.where(before, jnp.minimum(-z, 0.0) - sp, 0.0)
        hi = log_keep.astype(BF16)
        lo = (log_keep - hi.astype(F32)).astype(BF16)
        within = (jnp.dot(hi, later, preferred_element_type=F32)
                  + jnp.dot(lo, later, preferred_element_type=F32))
        a = jnp.where(before, jnp.exp(log_beta + within + c), 0.0)
        acc = acc + jnp.dot(a.astype(BF16), v_ref[pl.ds(koff, t), :], preferred_element_type=F32)
        c = c + log_keep.sum(axis=1, keepdims=True)
        return acc, c

    acc, _ = lax.fori_loop(0, i + 1, blk,
                           (jnp.zeros((t, HEAD_DIM), F32), jnp.zeros((t, 1), F32)))
    o_ref[...] = acc.astype(o_ref.dtype)


def _sb_attention(qkv, batch, seq):
    m = qkv.shape[0]
    t = SB_T
    assert seq % t == 0
    nq = seq // t
    return pl.pallas_call(
        _sb_kernel,
        grid=(batch, SB_HEADS, nq),
        in_specs=[pl.BlockSpec((t, HEAD_DIM), lambda b, h, i: (b * nq + i, h)),
                  pl.BlockSpec((seq, HEAD_DIM), lambda b, h, i: (b, SB_HEADS + h)),
                  pl.BlockSpec((seq, HEAD_DIM), lambda b, h, i: (b, 2 * SB_HEADS + h))],
        out_specs=pl.BlockSpec((t, HEAD_DIM), lambda b, h, i: (b * nq + i, h)),
        out_shape=jax.ShapeDtypeStruct((m, SB_HEADS * HEAD_DIM), BF16),
        compiler_params=_params(3),
        name="sb_attention",
    )(qkv, qkv, qkv)


def _sb_mixer(h, w_in, batch, seq):
    qkv = _mm_plain(h, w_in.astype(BF16), 0, 3 * SB_HEADS * HEAD_DIM)
    return _sb_attention(qkv, batch, seq)


def _trunk(x, c, layers, final_norm):
    batch, seq, d = x.shape
    c_pad = jnp.pad(c, ((0, SUBLANES - batch), (0, 0)))
    x2 = x.reshape(batch * seq, d)
    for kind, ada_w, ada_b, g_mix, g_ffn, mix, w_ff_in, w_ff_out in layers:
        mod3 = _adaln(c_pad, ada_w, ada_b).reshape(SUBLANES * 6, 1, d)
        h = _norm_mod(x2, g_mix, mod3, 1, 0, seq)
        if kind == 0:
            o = _dsa_mixer(h, mix[0], batch, seq)
        elif kind == 1:
            o = _mla_mixer(h, *mix[:5], batch, seq)
        else:
            o = _sb_mixer(h, mix[0], batch, seq)
        x2 = _mm_resid(o, mix[-1].astype(BF16), x2, mod3, 2, seq)
        h = _norm_mod(x2, g_ffn, mod3, 4, 3, seq)
        d_ff = w_ff_out.shape[0]
        u = _mm_swiglu(h, w_ff_in.astype(BF16), d_ff)
        x2 = _mm_resid(u, w_ff_out.astype(BF16), x2, mod3, 5, seq)
    return _norm_final(x2, final_norm).reshape(batch, seq, d)


def kernel(x, c, ada_w_0, ada_b_0, norm_mix_0, norm_ffn_0, dsa_w_in_0, dsa_w_out_0, ffn_w_in_0, ffn_w_out_0, ada_w_1, ada_b_1, norm_mix_1, norm_ffn_1, mla_w_down_1, mla_q_norm_1, mla_kv_norm_1, mla_w_uq_1, mla_w_ukv_1, mla_w_out_1, ffn_w_in_1, ffn_w_out_1, ada_w_2, ada_b_2, norm_mix_2, norm_ffn_2, sb_w_in_2, sb_w_out_2, ffn_w_in_2, ffn_w_out_2, ada_w_3, ada_b_3, norm_mix_3, norm_ffn_3, dsa_w_in_3, dsa_w_out_3, ffn_w_in_3, ffn_w_out_3, final_norm):
    layers = (
        (0, ada_w_0, ada_b_0, norm_mix_0, norm_ffn_0, (dsa_w_in_0, dsa_w_out_0), ffn_w_in_0, ffn_w_out_0),
        (1, ada_w_1, ada_b_1, norm_mix_1, norm_ffn_1,
         (mla_w_down_1, mla_q_norm_1, mla_kv_norm_1, mla_w_uq_1, mla_w_ukv_1, mla_w_out_1),
         ffn_w_in_1, ffn_w_out_1),
        (2, ada_w_2, ada_b_2, norm_mix_2, norm_ffn_2, (sb_w_in_2, sb_w_out_2), ffn_w_in_2, ffn_w_out_2),
        (0, ada_w_3, ada_b_3, norm_mix_3, norm_ffn_3, (dsa_w_in_3, dsa_w_out_3), ffn_w_in_3, ffn_w_out_3),
    )
    return _trunk(x, c, layers, final_norm)
```

```python
import functools
import math

import numpy as np
import jax
import jax.numpy as jnp
from jax import lax
from jax.experimental import pallas as pl
from jax.experimental.pallas import tpu as pltpu

F32 = jnp.float32
BF16 = jnp.bfloat16
I32 = jnp.int32

V7X_VMEM_BYTES = 64 * 1024 * 1024
VMEM_LIMIT_BYTES = V7X_VMEM_BYTES * 7 // 8
LANES = 128
SUBLANES = 8

CHUNK = 64
CHUNK_SHIFT = int(math.log2(CHUNK))
assert 1 << CHUNK_SHIFT == CHUNK
ROPE_THETA = 10000.0
NORM_EPS = 1e-6
DSA_HEADS = 16
DSA_KV_HEADS = 4
DSA_GROUP = DSA_HEADS // DSA_KV_HEADS
HEAD_DIM = 128
IDX_HEADS = 16
IDX_DIM = 128
IDX_ROPE_DIM = 64
TOPK_MAX = 256
MLA_HEADS = 16
MLA_Q_RANK = 512
MLA_KV_RANK = 512
MLA_NOPE_DIM = 128
MLA_ROPE_DIM = 64
MLA_V_DIM = 128
SB_HEADS = 16

NEG_BIG = -1e30
INT32_MIN = -(2 ** 31)
NEG_INF_KEY = int(np.int32(np.uint32(0xFF800000) ^ np.uint32(0x7FFFFFFF)))

_NT = (((1,), (1,)), ((), ()))


def _params(n_grid_dims):
    return pltpu.CompilerParams(
        dimension_semantics=("arbitrary",) * n_grid_dims,
        vmem_limit_bytes=VMEM_LIMIT_BYTES)


def _sigmoid(x):
    return 1.0 / (1.0 + jnp.exp(-x))


def _adaln_kernel(c_ref, w_ref, b_ref, o_ref):
    c = c_ref[...]
    s = (c * _sigmoid(c)).astype(BF16)
    o_ref[...] = jnp.dot(s, w_ref[...].astype(BF16), preferred_element_type=F32) + b_ref[...]


def _adaln(c_pad, w, b):
    rows, d = c_pad.shape
    n = w.shape[1]
    tn = 1024
    return pl.pallas_call(
        _adaln_kernel,
        grid=(n // tn,),
        in_specs=[pl.BlockSpec((rows, d), lambda j: (0, 0)),
                  pl.BlockSpec((d, tn), lambda j: (0, j)),
                  pl.BlockSpec((1, tn), lambda j: (0, j))],
        out_specs=pl.BlockSpec((rows, tn), lambda j: (0, j)),
        out_shape=jax.ShapeDtypeStruct((rows, n), F32),
        compiler_params=_params(1),
        name="adaln",
    )(c_pad, w, b.reshape(1, n))


def _rms(x, g):
    return x * lax.rsqrt(jnp.mean(x * x, axis=-1, keepdims=True) + NORM_EPS) * g


def _norm_mod_kernel(x_ref, g_ref, sc_ref, sh_ref, o_ref):
    y = _rms(x_ref[...], g_ref[...])
    o_ref[...] = (y * (1.0 + sc_ref[...]) + sh_ref[...]).astype(o_ref.dtype)


def _norm_kernel(x_ref, g_ref, o_ref):
    o_ref[...] = _rms(x_ref[...], g_ref[...]).astype(o_ref.dtype)


def _norm_mod(x2, g, mod3, sc_idx, sh_idx, seq):
    m, d = x2.shape
    ts = 512
    per = seq // ts
    return pl.pallas_call(
        _norm_mod_kernel,
        grid=(m // ts,),
        in_specs=[pl.BlockSpec((ts, d), lambda i: (i, 0)),
                  pl.BlockSpec((1, d), lambda i: (0, 0)),
                  pl.BlockSpec((None, 1, d), lambda i: ((i // per) * 6 + sc_idx, 0, 0)),
                  pl.BlockSpec((None, 1, d), lambda i: ((i // per) * 6 + sh_idx, 0, 0))],
        out_specs=pl.BlockSpec((ts, d), lambda i: (i, 0)),
        out_shape=jax.ShapeDtypeStruct((m, d), BF16),
        compiler_params=_params(1),
        name="norm_mod",
    )(x2, g.reshape(1, d), mod3, mod3)


def _norm_final(x2, g):
    m, d = x2.shape
    ts = 512
    return pl.pallas_call(
        _norm_kernel,
        grid=(m // ts,),
        in_specs=[pl.BlockSpec((ts, d), lambda i: (i, 0)),
                  pl.BlockSpec((1, d), lambda i: (0, 0))],
        out_specs=pl.BlockSpec((ts, d), lambda i: (i, 0)),
        out_shape=jax.ShapeDtypeStruct((m, d), F32),
        compiler_params=_params(1),
        name="norm_final",
    )(x2, g.reshape(1, d))


def _dot(a_ref, w_ref):
    return jnp.dot(a_ref[...], w_ref[...], preferred_element_type=F32)


def _mm_plain_kernel(a_ref, w_ref, o_ref):
    o_ref[...] = _dot(a_ref, w_ref).astype(o_ref.dtype)


def _mm_rope_kernel(a_ref, w_ref, t0_ref, t1_ref, t2_ref, o_ref, *, kinds, shift):
    acc = _dot(a_ref, w_ref)
    period = len(kinds)
    for c in range(acc.shape[1] // LANES):
        y = acc[:, c * LANES:(c + 1) * LANES]
        k = c % period
        if kinds[k]:
            t = slice(k * LANES, (k + 1) * LANES)
            out = y * t0_ref[:, t] + pltpu.roll(y, LANES - shift, 1) * t1_ref[:, t]
            if 2 * shift != LANES:
                out = out + pltpu.roll(y, shift, 1) * t2_ref[:, t]
        else:
            out = y
        o_ref[:, c * LANES:(c + 1) * LANES] = out.astype(o_ref.dtype)


def _mm_rms_kernel(a_ref, w_ref, g_ref, o_ref):
    o_ref[...] = _rms(_dot(a_ref, w_ref), g_ref[...]).astype(o_ref.dtype)


def _mm_swiglu_kernel(a_ref, wg_ref, wu_ref, o_ref):
    a = a_ref[...]
    g = jnp.dot(a, wg_ref[...], preferred_element_type=F32)
    u = jnp.dot(a, wu_ref[...], preferred_element_type=F32)
    o_ref[...] = (g * _sigmoid(g) * u).astype(o_ref.dtype)


def _mm_resid_kernel(a_ref, w_ref, x_ref, gt_ref, o_ref):
    o_ref[...] = x_ref[...] + gt_ref[...] * _dot(a_ref, w_ref)


def _mm_call(kernel, a, weights, w_offs, extras, extra_specs, n_out, out_dtype, tm, tn, name):
    m, k = a.shape
    assert m % tm == 0 and n_out % tn == 0
    in_specs = [pl.BlockSpec((tm, k), lambda j, i: (i, 0))]
    for off in w_offs:
        assert off % tn == 0
        in_specs.append(pl.BlockSpec((k, tn), lambda j, i, o=off // tn: (0, j + o)))
    in_specs += extra_specs
    return pl.pallas_call(
        kernel,
        grid=(n_out // tn, m // tm),
        in_specs=in_specs,
        out_specs=pl.BlockSpec((tm, tn), lambda j, i: (i, j)),
        out_shape=jax.ShapeDtypeStruct((m, n_out), out_dtype),
        compiler_params=_params(2),
        name=name,
    )(a, *weights, *extras)


def _mm_plain(a, w, col_off, n_out, out_dtype=BF16, tm=1024, tn=512):
    tn = min(tn, n_out)
    return _mm_call(_mm_plain_kernel, a, [w], [col_off], [], [], n_out, out_dtype, tm, tn, "mm_plain")


def _mm_rope(a, w, col_off, n_out, tables, kinds, shift, seq, tm=1024, tn=512):
    tn = min(tn, n_out)
    per = seq // tm
    width = tables[0].shape[1]
    assert width == LANES * len(kinds) and tn % width == 0
    specs = [pl.BlockSpec((tm, width), lambda j, i: (i % per, 0))] * 3
    kern = functools.partial(_mm_rope_kernel, kinds=kinds, shift=shift)
    return _mm_call(kern, a, [w], [col_off], list(tables), specs, n_out, BF16, tm, tn, "mm_rope")


def _mm_rms(a, w, col_off, g, tm=1024):
    n = g.shape[0]
    specs = [pl.BlockSpec((1, n), lambda j, i: (0, 0))]
    return _mm_call(_mm_rms_kernel, a, [w], [col_off], [g.reshape(1, n)], specs, n, BF16, tm, n, "mm_rms")


def _mm_swiglu(a, w, d_ff, tm=1024, tn=512):
    return _mm_call(_mm_swiglu_kernel, a, [w, w], [0, d_ff], [], [], d_ff, BF16, tm, tn, "mm_swiglu")


def _mm_resid(a, w, x2, mod3, gate_idx, seq, tm=512, tn=512):
    per = seq // tm
    specs = [pl.BlockSpec((tm, tn), lambda j, i: (i, j)),
             pl.BlockSpec((None, 1, tn), lambda j, i: ((i // per) * 6 + gate_idx, 0, j))]
    return _mm_call(_mm_resid_kernel, a, [w], [0], [x2, mod3], specs, w.shape[1], F32, tm, tn, "mm_resid")


def _rope_tables(seq, rot_dim, pad_to):
    half = rot_dim // 2
    inv_freq = ROPE_THETA ** (-jnp.arange(0, rot_dim, 2, dtype=F32) / rot_dim)
    ang = jnp.arange(seq, dtype=F32)[:, None] * inv_freq[None, :]
    cos, sin = jnp.cos(ang), jnp.sin(ang)
    zero = jnp.zeros_like(sin)
    tail1 = jnp.ones((seq, pad_to - rot_dim), F32)
    tail0 = jnp.zeros((seq, pad_to - rot_dim), F32)
    if 2 * half == pad_to == LANES:
        return (jnp.concatenate([cos, cos], 1), jnp.concatenate([-sin, sin], 1),
                jnp.zeros((seq, pad_to), F32))
    t0 = jnp.concatenate([cos, cos, tail1], 1)
    t1 = jnp.concatenate([-sin, zero, tail0], 1)
    t2 = jnp.concatenate([zero, sin, tail0], 1)
    return t0, t1, t2


DSA_QB = 128
DSA_KB = 512


def _sortable_key(x):
    b = lax.bitcast_convert_type(x, I32)
    return b ^ ((b >> 31) & 0x7FFFFFFF)


def _dsa_kernel(q_ref, iq_ref, wt_ref, k_ref, v_ref, ik_ref, o_ref,
                keys_ref, bias_ref, qg_ref, acc_ref, m_ref, l_ref, *, top_k):
    qb, kb_sz = DSA_QB, DSA_KB
    start = pl.program_id(1) * qb
    n_blk = (start + qb + kb_sz - 1) // kb_sz

    w_scaled = wt_ref[...] * (IDX_DIM ** -0.5 * IDX_HEADS ** -0.5)
    q_chunk = (start + lax.broadcasted_iota(I32, (kb_sz, qb), 1)) >> CHUNK_SHIFT
    k_iota = lax.broadcasted_iota(I32, (kb_sz, qb), 0)

    def score_blk(kb, carry):
        koff = pl.multiple_of(kb * kb_sz, kb_sz)
        ikb = ik_ref[pl.ds(koff, kb_sz), :]
        acc = jnp.zeros((kb_sz, qb), F32)
        for h in range(IDX_HEADS):
            r = lax.dot_general(ikb, iq_ref[:, h * IDX_DIM:(h + 1) * IDX_DIM], _NT,
                                preferred_element_type=F32)
            acc = acc + jnp.maximum(r, 0.0) * w_scaled[h:h + 1, :]
        visible = ((koff + k_iota) >> CHUNK_SHIFT) <= q_chunk
        keys_ref[pl.ds(koff, kb_sz), :] = _sortable_key(jnp.where(visible, acc, -jnp.inf))
        return carry

    lax.fori_loop(0, n_blk, score_blk, 0)

    def bit_step(b, thr):
        cand = thr + lax.shift_left(jnp.int32(1), 31 - b)

        def count_blk(kb, cnt):
            koff = pl.multiple_of(kb * kb_sz, kb_sz)
            ge = (keys_ref[pl.ds(koff, kb_sz), :] >= cand).astype(I32)
            return cnt + ge.reshape(kb_sz // SUBLANES, SUBLANES, qb).sum(axis=0)

        cnt = lax.fori_loop(0, n_blk, count_blk, jnp.zeros((SUBLANES, qb), I32))
        total = cnt.sum(axis=0, keepdims=True)
        return jnp.where(total >= top_k, cand, thr)

    thr = lax.fori_loop(0, 32, bit_step, jnp.full((1, qb), INT32_MIN, I32))

    def bias_blk(kb, carry):
        koff = pl.multiple_of(kb * kb_sz, kb_sz)
        blk = keys_ref[pl.ds(koff, kb_sz), :]
        bias_t = jnp.where(blk >= thr, jnp.where(blk > NEG_INF_KEY, 0.0, NEG_BIG), NEG_BIG)
        bias_ref[kb] = bias_t.astype(F32).T
        return carry

    lax.fori_loop(0, n_blk, bias_blk, 0)

    scale = HEAD_DIM ** -0.5
    for g in range(DSA_KV_HEADS):
        for r in range(DSA_GROUP):
            h = g * DSA_GROUP + r
            qg_ref[g, r * qb:(r + 1) * qb, :] = q_ref[:, h * HEAD_DIM:(h + 1) * HEAD_DIM]
    m_ref[...] = jnp.full(m_ref.shape, NEG_BIG, F32)
    l_ref[...] = jnp.zeros(l_ref.shape, F32)
    acc_ref[...] = jnp.zeros(acc_ref.shape, F32)

    def att_blk(kb, carry):
        koff = pl.multiple_of(kb * kb_sz, kb_sz)
        bias = bias_ref[kb][None]
        for g in range(DSA_KV_HEADS):
            kblk = k_ref[pl.ds(koff, kb_sz), g * HEAD_DIM:(g + 1) * HEAD_DIM]
            vblk = v_ref[pl.ds(koff, kb_sz), g * HEAD_DIM:(g + 1) * HEAD_DIM]
            s = lax.dot_general(qg_ref[g], kblk, _NT, preferred_element_type=F32) * scale
            s = (s.reshape(DSA_GROUP, qb, kb_sz) + bias).reshape(DSA_GROUP * qb, kb_sz)
            m_prev = m_ref[g]
            m_new = jnp.maximum(m_prev, s.max(axis=1, keepdims=True))
            alpha = jnp.exp(m_prev - m_new)
            p = jnp.exp(s - m_new)
            l_ref[g] = alpha * l_ref[g] + p.sum(axis=1, keepdims=True)
            acc_ref[g] = alpha * acc_ref[g] + jnp.dot(p.astype(BF16), vblk,
                                                      preferred_element_type=F32)
            m_ref[g] = m_new
        return carry

    lax.fori_loop(0, n_blk, att_blk, 0)
    for g in range(DSA_KV_HEADS):
        o = acc_ref[g] / l_ref[g]
        for r in range(DSA_GROUP):
            h = g * DSA_GROUP + r
            o_ref[:, h * HEAD_DIM:(h + 1) * HEAD_DIM] = o[r * qb:(r + 1) * qb].astype(o_ref.dtype)


def _dsa_attention(qk, v, iq, ik, iw_t, batch, seq):
    m = qk.shape[0]
    qb, kb_sz = DSA_QB, DSA_KB
    assert seq % kb_sz == 0
    nq = seq // qb
    top_k = min(TOPK_MAX, seq // 4)
    d_q = DSA_HEADS * HEAD_DIM
    d_kv = DSA_KV_HEADS * HEAD_DIM
    return pl.pallas_call(
        functools.partial(_dsa_kernel, top_k=top_k),
        grid=(batch, nq),
        in_specs=[pl.BlockSpec((qb, d_q), lambda b, j: (b * nq + j, 0)),
                  pl.BlockSpec((qb, IDX_HEADS * IDX_DIM), lambda b, j: (b * nq + j, 0)),
                  pl.BlockSpec((IDX_HEADS, qb), lambda b, j: (b, j)),
                  pl.BlockSpec((seq, d_kv), lambda b, j: (b, d_q // d_kv)),
                  pl.BlockSpec((seq, d_kv), lambda b, j: (b, 0)),
                  pl.BlockSpec((seq, IDX_DIM), lambda b, j: (b, 0))],
        out_specs=pl.BlockSpec((qb, d_q), lambda b, j: (b * nq + j, 0)),
        out_shape=jax.ShapeDtypeStruct((m, d_q), BF16),
        scratch_shapes=[pltpu.VMEM((seq, qb), I32),
                        pltpu.VMEM((seq // kb_sz, qb, kb_sz), F32),
                        pltpu.VMEM((DSA_KV_HEADS, DSA_GROUP * qb, HEAD_DIM), BF16),
                        pltpu.VMEM((DSA_KV_HEADS, DSA_GROUP * qb, HEAD_DIM), F32),
                        pltpu.VMEM((DSA_KV_HEADS, DSA_GROUP * qb, 1), F32),
                        pltpu.VMEM((DSA_KV_HEADS, DSA_GROUP * qb, 1), F32)],
        compiler_params=_params(2),
        name="dsa_attention",
    )(qk, iq, iw_t, qk, v, ik)


def _dsa_mixer(h, w_in, batch, seq):
    d_q = DSA_HEADS * HEAD_DIM
    d_kv = DSA_KV_HEADS * HEAD_DIM
    d_iq = IDX_HEADS * IDX_DIM
    off_v = d_q + d_kv
    off_iq = off_v + d_kv
    off_ik = off_iq + d_iq
    off_iw = off_ik + IDX_DIM
    w = w_in.astype(BF16)
    w_iw = jnp.pad(w[:, off_iw:], ((0, 0), (0, LANES - IDX_HEADS)))
    full_tabs = _rope_tables(seq, HEAD_DIM, HEAD_DIM)
    idx_tabs = _rope_tables(seq, IDX_ROPE_DIM, IDX_DIM)
    qk = _mm_rope(h, w, 0, d_q + d_kv, full_tabs, (True,), HEAD_DIM // 2, seq)
    v = _mm_plain(h, w, off_v, d_kv)
    iq = _mm_rope(h, w, off_iq, d_iq, idx_tabs, (True,), IDX_ROPE_DIM // 2, seq)
    ik = _mm_rope(h, w, off_ik, IDX_DIM, idx_tabs, (True,), IDX_ROPE_DIM // 2, seq)
    iw = _mm_plain(h, w_iw, 0, LANES, out_dtype=F32)
    iw_t = iw[:, :IDX_HEADS].reshape(batch, seq, IDX_HEADS).transpose(0, 2, 1)
    iw_t = iw_t.reshape(batch * IDX_HEADS, seq)
    return _dsa_attention(qk, v, iq, ik, iw_t, batch, seq)


MLA_TQ = 256
MLA_TK = 512
MLA_HP = 4
MLA_QK_PAD = 2 * LANES
MLA_KV_W = MLA_NOPE_DIM + MLA_V_DIM


def _mla_kernel(q_ref, kv_ref, kr_ref, o_ref, acc_ref, m_ref, l_ref):
    tq, tk = MLA_TQ, MLA_TK
    start = pl.program_id(2) * tq
    n_blk = (start + tq + tk - 1) // tk
    n_full = (start + CHUNK) // tk
    scale = (MLA_NOPE_DIM + MLA_ROPE_DIM) ** -0.5
    q_chunk = (start + lax.broadcasted_iota(I32, (tq, tk), 0)) >> CHUNK_SHIFT
    k_iota = lax.broadcasted_iota(I32, (tq, tk), 1)
    m_ref[...] = jnp.full(m_ref.shape, NEG_BIG, F32)
    l_ref[...] = jnp.zeros(l_ref.shape, F32)
    acc_ref[...] = jnp.zeros(acc_ref.shape, F32)

    def blk(kb, carry, masked):
        koff = pl.multiple_of(kb * tk, tk)
        kr = kr_ref[pl.ds(koff, tk), :]
        if masked:
            visible = ((koff + k_iota) >> CHUNK_SHIFT) <= q_chunk
        for hh in range(MLA_HP):
            kn = kv_ref[pl.ds(koff, tk), hh * MLA_KV_W:hh * MLA_KV_W + MLA_NOPE_DIM]
            vb = kv_ref[pl.ds(koff, tk), hh * MLA_KV_W + MLA_NOPE_DIM:(hh + 1) * MLA_KV_W]
            q = q_ref[:, hh * MLA_QK_PAD:(hh + 1) * MLA_QK_PAD]
            s = lax.dot_general(q, jnp.concatenate([kn, kr], axis=1), _NT,
                                preferred_element_type=F32) * scale
            if masked:
                s = jnp.where(visible, s, NEG_BIG)
            m_prev = m_ref[hh]
            m_new = jnp.maximum(m_prev, s.max(axis=1, keepdims=True))
            alpha = jnp.exp(m_prev - m_new)
            p = jnp.exp(s - m_new)
            l_ref[hh] = alpha * l_ref[hh] + p.sum(axis=1, keepdims=True)
            acc_ref[hh] = alpha * acc_ref[hh] + jnp.dot(p.astype(BF16), vb,
                                                        preferred_element_type=F32)
            m_ref[hh] = m_new
        return carry

    lax.fori_loop(0, n_full, functools.partial(blk, masked=False), 0)
    lax.fori_loop(n_full, n_blk, functools.partial(blk, masked=True), 0)
    for hh in range(MLA_HP):
        o_ref[:, hh * MLA_V_DIM:(hh + 1) * MLA_V_DIM] = (acc_ref[hh] / l_ref[hh]).astype(o_ref.dtype)


def _mla_attention(q, kv, k_rope, batch, seq):
    m = q.shape[0]
    tq, tk, hp = MLA_TQ, MLA_TK, MLA_HP
    assert seq % tk == 0 and seq % tq == 0 and MLA_HEADS % hp == 0
    nq = seq // tq
    return pl.pallas_call(
        _mla_kernel,
        grid=(batch, MLA_HEADS // hp, nq),
        in_specs=[pl.BlockSpec((tq, hp * MLA_QK_PAD), lambda b, h, i: (b * nq + i, h)),
                  pl.BlockSpec((seq, hp * MLA_KV_W), lambda b, h, i: (b, h)),
                  pl.BlockSpec((seq, LANES), lambda b, h, i: (b, 0))],
        out_specs=pl.BlockSpec((tq, hp * MLA_V_DIM), lambda b, h, i: (b * nq + i, h)),
        out_shape=jax.ShapeDtypeStruct((m, MLA_HEADS * MLA_V_DIM), BF16),
        scratch_shapes=[pltpu.VMEM((hp, tq, MLA_V_DIM), F32),
                        pltpu.VMEM((hp, tq, 1), F32),
                        pltpu.VMEM((hp, tq, 1), F32)],
        compiler_params=_params(3),
        name="mla_attention",
    )(q, kv, k_rope)


def _mla_mixer(h, w_down, q_norm, kv_norm, w_uq, w_ukv, batch, seq):
    wd = w_down.astype(BF16)
    wd_rope = jnp.pad(wd[:, MLA_Q_RANK + MLA_KV_RANK:], ((0, 0), (0, LANES - MLA_ROPE_DIM)))
    w_uq_h = w_uq.astype(BF16).reshape(MLA_Q_RANK, MLA_HEADS, MLA_NOPE_DIM + MLA_ROPE_DIM)
    pad = MLA_QK_PAD - MLA_NOPE_DIM - MLA_ROPE_DIM
    w_uq_p = jnp.pad(w_uq_h, ((0, 0), (0, 0), (0, pad))).reshape(MLA_Q_RANK, MLA_HEADS * MLA_QK_PAD)
    rope_tabs = _rope_tables(seq, MLA_ROPE_DIM, LANES)
    q_tabs = tuple(jnp.concatenate([jnp.zeros_like(t), t], axis=1) for t in rope_tabs)

    cq = _mm_rms(h, wd, 0, q_norm)
    ckv = _mm_rms(h, wd, MLA_Q_RANK, kv_norm)
    k_rope = _mm_rope(h, wd_rope, 0, LANES, rope_tabs, (True,), MLA_ROPE_DIM // 2, seq)
    q = _mm_rope(cq, w_uq_p, 0, MLA_HEADS * MLA_QK_PAD, q_tabs, (False, True), MLA_ROPE_DIM // 2, seq)
    kv = _mm_plain(ckv, w_ukv.astype(BF16), 0, MLA_HEADS * (MLA_NOPE_DIM + MLA_V_DIM))
    return _mla_attention(q, kv, k_rope, batch, seq)


SB_T = 256
SB_HP = 4
SB_EXP_ZERO_BELOW = -104.0


def _sb_kernel(q_ref, k_ref, v_ref, o_ref, acc_ref, c_ref):
    t = SB_T
    i = pl.program_id(2)
    scale = HEAD_DIM ** -0.5
    row = lax.broadcasted_iota(I32, (t, t), 0)
    col = lax.broadcasted_iota(I32, (t, t), 1)
    later = (row > col).astype(BF16)
    acc_ref[...] = jnp.zeros(acc_ref.shape, F32)
    c_ref[...] = jnp.zeros(c_ref.shape, F32)

    def step(n, diagonal):
        koff = pl.multiple_of((i - n) * t, t)
        c_max = jnp.float32(-jnp.inf)
        for hh in range(SB_HP):
            lanes = slice(hh * HEAD_DIM, (hh + 1) * HEAD_DIM)
            z = lax.dot_general(q_ref[:, lanes], k_ref[pl.ds(koff, t), lanes], _NT,
                                preferred_element_type=F32) * scale
            sp = jnp.log1p(jnp.exp(-jnp.abs(z)))
            log_beta = jnp.minimum(z, 0.0) - sp
            log_keep = jnp.minimum(-z, 0.0) - sp
            if diagonal:
                before = col < row
                log_keep = jnp.where(before, log_keep, 0.0)
            hi = log_keep.astype(BF16)
            lo = (log_keep - hi.astype(F32)).astype(BF16)
            within = (jnp.dot(hi, later, preferred_element_type=F32)
                      + jnp.dot(lo, later, preferred_element_type=F32))
            c = c_ref[hh]
            a = jnp.exp(log_beta + within + c)
            if diagonal:
                a = jnp.where(before, a, 0.0)
            acc_ref[hh] += jnp.dot(a.astype(BF16), v_ref[pl.ds(koff, t), lanes],
                                   preferred_element_type=F32)
            c = c + log_keep.sum(axis=1, keepdims=True)
            c_ref[hh] = c
            c_max = jnp.maximum(c_max, jnp.max(c))
        return c_max

    def cond(state):
        n, c_max = state
        return jnp.logical_and(n <= i, c_max >= SB_EXP_ZERO_BELOW)

    def body(state):
        n, _ = state
        return n + 1, step(n, diagonal=False)

    lax.while_loop(cond, body, (jnp.int32(1), step(0, diagonal=True)))
    for hh in range(SB_HP):
        o_ref[:, hh * HEAD_DIM:(hh + 1) * HEAD_DIM] = acc_ref[hh].astype(o_ref.dtype)


def _sb_attention(qkv, batch, seq):
    m = qkv.shape[0]
    t, hp = SB_T, SB_HP
    assert seq % t == 0 and SB_HEADS % hp == 0
    nq = seq // t
    ng = SB_HEADS // hp
    w = hp * HEAD_DIM
    return pl.pallas_call(
        _sb_kernel,
        grid=(batch, ng, nq),
        in_specs=[pl.BlockSpec((t, w), lambda b, h, i: (b * nq + i, h)),
                  pl.BlockSpec((seq, w), lambda b, h, i: (b, ng + h)),
                  pl.BlockSpec((seq, w), lambda b, h, i: (b, 2 * ng + h))],
        out_specs=pl.BlockSpec((t, w), lambda b, h, i: (b * nq + i, h)),
        out_shape=jax.ShapeDtypeStruct((m, SB_HEADS * HEAD_DIM), BF16),
        scratch_shapes=[pltpu.VMEM((hp, t, HEAD_DIM), F32),
                        pltpu.VMEM((hp, t, 1), F32)],
        compiler_params=_params(3),
        name="sb_attention",
    )(qkv, qkv, qkv)


def _sb_mixer(h, w_in, batch, seq):
    qkv = _mm_plain(h, w_in.astype(BF16), 0, 3 * SB_HEADS * HEAD_DIM)
    return _sb_attention(qkv, batch, seq)


def _trunk(x, c, layers, final_norm):
    batch, seq, d = x.shape
    c_pad = jnp.pad(c, ((0, SUBLANES - batch), (0, 0)))
    x2 = x.reshape(batch * seq, d)
    for kind, ada_w, ada_b, g_mix, g_ffn, mix, w_ff_in, w_ff_out in layers:
        mod3 = _adaln(c_pad, ada_w, ada_b).reshape(SUBLANES * 6, 1, d)
        h = _norm_mod(x2, g_mix, mod3, 1, 0, seq)
        if kind == 0:
            o = _dsa_mixer(h, mix[0], batch, seq)
        elif kind == 1:
            o = _mla_mixer(h, *mix[:5], batch, seq)
        else:
            o = _sb_mixer(h, mix[0], batch, seq)
        x2 = _mm_resid(o, mix[-1].astype(BF16), x2, mod3, 2, seq)
        h = _norm_mod(x2, g_ffn, mod3, 4, 3, seq)
        d_ff = w_ff_out.shape[0]
        u = _mm_swiglu(h, w_ff_in.astype(BF16), d_ff)
        x2 = _mm_resid(u, w_ff_out.astype(BF16), x2, mod3, 5, seq)
    return _norm_final(x2, final_norm).reshape(batch, seq, d)


def kernel(x, c, ada_w_0, ada_b_0, norm_mix_0, norm_ffn_0, dsa_w_in_0, dsa_w_out_0, ffn_w_in_0, ffn_w_out_0, ada_w_1, ada_b_1, norm_mix_1, norm_ffn_1, mla_w_down_1, mla_q_norm_1, mla_kv_norm_1, mla_w_uq_1, mla_w_ukv_1, mla_w_out_1, ffn_w_in_1, ffn_w_out_1, ada_w_2, ada_b_2, norm_mix_2, norm_ffn_2, sb_w_in_2, sb_w_out_2, ffn_w_in_2, ffn_w_out_2, ada_w_3, ada_b_3, norm_mix_3, norm_ffn_3, dsa_w_in_3, dsa_w_out_3, ffn_w_in_3, ffn_w_out_3, final_norm):
    layers = (
        (0, ada_w_0, ada_b_0, norm_mix_0, norm_ffn_0, (dsa_w_in_0, dsa_w_out_0), ffn_w_in_0, ffn_w_out_0),
        (1, ada_w_1, ada_b_1, norm_mix_1, norm_ffn_1,
         (mla_w_down_1, mla_q_norm_1, mla_kv_norm_1, mla_w_uq_1, mla_w_ukv_1, mla_w_out_1),
         ffn_w_in_1, ffn_w_out_1),
        (2, ada_w_2, ada_b_2, norm_mix_2, norm_ffn_2, (sb_w_in_2, sb_w_out_2), ffn_w_in_2, ffn_w_out_2),
        (0, ada_w_3, ada_b_3, norm_mix_3, norm_ffn_3, (dsa_w_in_3, dsa_w_out_3), ffn_w_in_3, ffn_w_out_3),
    )
    return _trunk(x, c, layers, final_norm)
```

```python
import functools
import math

import numpy as np
import jax
import jax.numpy as jnp
from jax import lax
from jax.experimental import pallas as pl
from jax.experimental.pallas import tpu as pltpu

F32 = jnp.float32
BF16 = jnp.bfloat16
I32 = jnp.int32

V7X_VMEM_BYTES = 64 * 1024 * 1024
VMEM_LIMIT_BYTES = V7X_VMEM_BYTES * 7 // 8
LANES = 128
SUBLANES = 8

CHUNK = 64
CHUNK_SHIFT = int(math.log2(CHUNK))
assert 1 << CHUNK_SHIFT == CHUNK
ROPE_THETA = 10000.0
NORM_EPS = 1e-6
DSA_HEADS = 16
DSA_KV_HEADS = 4
DSA_GROUP = DSA_HEADS // DSA_KV_HEADS
HEAD_DIM = 128
IDX_HEADS = 16
IDX_DIM = 128
IDX_ROPE_DIM = 64
TOPK_MAX = 256
MLA_HEADS = 16
MLA_Q_RANK = 512
MLA_KV_RANK = 512
MLA_NOPE_DIM = 128
MLA_ROPE_DIM = 64
MLA_V_DIM = 128
SB_HEADS = 16

NEG_BIG = -1e30
INT32_MIN = -(2 ** 31)
NEG_INF_KEY = int(np.int32(np.uint32(0xFF800000) ^ np.uint32(0x7FFFFFFF)))

_NT = (((1,), (1,)), ((), ()))


def _params(n_grid_dims):
    return pltpu.CompilerParams(
        dimension_semantics=("arbitrary",) * n_grid_dims,
        vmem_limit_bytes=VMEM_LIMIT_BYTES)


def _sigmoid(x):
    return 1.0 / (1.0 + jnp.exp(-x))


def _adaln_kernel(c_ref, w_ref, b_ref, o_ref):
    c = c_ref[...]
    s = (c * _sigmoid(c)).astype(BF16)
    o_ref[...] = jnp.dot(s, w_ref[...].astype(BF16), preferred_element_type=F32) + b_ref[...]


def _adaln(c_pad, w, b):
    rows, d = c_pad.shape
    n = w.shape[1]
    tn = 1024
    return pl.pallas_call(
        _adaln_kernel,
        grid=(n // tn,),
        in_specs=[pl.BlockSpec((rows, d), lambda j: (0, 0)),
                  pl.BlockSpec((d, tn), lambda j: (0, j)),
                  pl.BlockSpec((1, tn), lambda j: (0, j))],
        out_specs=pl.BlockSpec((rows, tn), lambda j: (0, j)),
        out_shape=jax.ShapeDtypeStruct((rows, n), F32),
        compiler_params=_params(1),
        name="adaln",
    )(c_pad, w, b.reshape(1, n))


def _rms(x, g):
    return x * lax.rsqrt(jnp.mean(x * x, axis=-1, keepdims=True) + NORM_EPS) * g


def _norm_mod_kernel(x_ref, g_ref, sc_ref, sh_ref, o_ref):
    y = _rms(x_ref[...], g_ref[...])
    o_ref[...] = (y * (1.0 + sc_ref[...]) + sh_ref[...]).astype(o_ref.dtype)


def _norm_kernel(x_ref, g_ref, o_ref):
    o_ref[...] = _rms(x_ref[...], g_ref[...]).astype(o_ref.dtype)


def _norm_mod(x2, g, mod3, sc_idx, sh_idx, seq):
    m, d = x2.shape
    ts = 512
    per = seq // ts
    return pl.pallas_call(
        _norm_mod_kernel,
        grid=(m // ts,),
        in_specs=[pl.BlockSpec((ts, d), lambda i: (i, 0)),
                  pl.BlockSpec((1, d), lambda i: (0, 0)),
                  pl.BlockSpec((None, 1, d), lambda i: ((i // per) * 6 + sc_idx, 0, 0)),
                  pl.BlockSpec((None, 1, d), lambda i: ((i // per) * 6 + sh_idx, 0, 0))],
        out_specs=pl.BlockSpec((ts, d), lambda i: (i, 0)),
        out_shape=jax.ShapeDtypeStruct((m, d), BF16),
        compiler_params=_params(1),
        name="norm_mod",
    )(x2, g.reshape(1, d), mod3, mod3)


def _norm_final(x2, g):
    m, d = x2.shape
    ts = 512
    return pl.pallas_call(
        _norm_kernel,
        grid=(m // ts,),
        in_specs=[pl.BlockSpec((ts, d), lambda i: (i, 0)),
                  pl.BlockSpec((1, d), lambda i: (0, 0))],
        out_specs=pl.BlockSpec((ts, d), lambda i: (i, 0)),
        out_shape=jax.ShapeDtypeStruct((m, d), F32),
        compiler_params=_params(1),
        name="norm_final",
    )(x2, g.reshape(1, d))


def _dot(a_ref, w_ref):
    return jnp.dot(a_ref[...], w_ref[...], preferred_element_type=F32)


def _mm_plain_kernel(a_ref, w_ref, o_ref):
    o_ref[...] = _dot(a_ref, w_ref).astype(o_ref.dtype)


def _mm_rope_kernel(a_ref, w_ref, t0_ref, t1_ref, t2_ref, o_ref, *, kinds, shift):
    acc = _dot(a_ref, w_ref)
    period = len(kinds)
    for c in range(acc.shape[1] // LANES):
        y = acc[:, c * LANES:(c + 1) * LANES]
        k = c % period
        if kinds[k]:
            t = slice(k * LANES, (k + 1) * LANES)
            out = y * t0_ref[:, t] + pltpu.roll(y, LANES - shift, 1) * t1_ref[:, t]
            if 2 * shift != LANES:
                out = out + pltpu.roll(y, shift, 1) * t2_ref[:, t]
        else:
            out = y
        o_ref[:, c * LANES:(c + 1) * LANES] = out.astype(o_ref.dtype)


def _mm_rms_kernel(a_ref, w_ref, g_ref, o_ref, *, out_scale):
    o_ref[...] = (_rms(_dot(a_ref, w_ref), g_ref[...]) * out_scale).astype(o_ref.dtype)


def _mm_swiglu_kernel(a_ref, wg_ref, wu_ref, o_ref):
    a = a_ref[...]
    g = jnp.dot(a, wg_ref[...], preferred_element_type=F32)
    u = jnp.dot(a, wu_ref[...], preferred_element_type=F32)
    o_ref[...] = (g * _sigmoid(g) * u).astype(o_ref.dtype)


def _mm_resid_kernel(a_ref, w_ref, x_ref, gt_ref, o_ref):
    o_ref[...] = x_ref[...] + gt_ref[...] * _dot(a_ref, w_ref)


def _mm_call(kernel, a, weights, w_offs, extras, extra_specs, n_out, out_dtype, tm, tn, name):
    m, k = a.shape
    assert m % tm == 0 and n_out % tn == 0
    in_specs = [pl.BlockSpec((tm, k), lambda j, i: (i, 0))]
    for off in w_offs:
        assert off % tn == 0
        in_specs.append(pl.BlockSpec((k, tn), lambda j, i, o=off // tn: (0, j + o)))
    in_specs += extra_specs
    return pl.pallas_call(
        kernel,
        grid=(n_out // tn, m // tm),
        in_specs=in_specs,
        out_specs=pl.BlockSpec((tm, tn), lambda j, i: (i, j)),
        out_shape=jax.ShapeDtypeStruct((m, n_out), out_dtype),
        compiler_params=_params(2),
        name=name,
    )(a, *weights, *extras)


def _mm_plain(a, w, col_off, n_out, out_dtype=BF16, tm=1024, tn=512):
    tn = min(tn, n_out)
    return _mm_call(_mm_plain_kernel, a, [w], [col_off], [], [], n_out, out_dtype, tm, tn, "mm_plain")


def _mm_rope(a, w, col_off, n_out, tables, kinds, shift, seq, tm=1024, tn=512):
    tn = min(tn, n_out)
    per = seq // tm
    width = tables[0].shape[1]
    assert width == LANES * len(kinds) and tn % width == 0
    specs = [pl.BlockSpec((tm, width), lambda j, i: (i % per, 0))] * 3
    kern = functools.partial(_mm_rope_kernel, kinds=kinds, shift=shift)
    return _mm_call(kern, a, [w], [col_off], list(tables), specs, n_out, BF16, tm, tn, "mm_rope")


def _mm_rms(a, w, col_off, g, out_scale=1.0, tm=1024):
    n = g.shape[0]
    specs = [pl.BlockSpec((1, n), lambda j, i: (0, 0))]
    kern = functools.partial(_mm_rms_kernel, out_scale=out_scale)
    return _mm_call(kern, a, [w], [col_off], [g.reshape(1, n)], specs, n, BF16, tm, n, "mm_rms")


def _mm_swiglu(a, w, d_ff, tm=1024, tn=512):
    return _mm_call(_mm_swiglu_kernel, a, [w, w], [0, d_ff], [], [], d_ff, BF16, tm, tn, "mm_swiglu")


def _mm_resid(a, w, x2, mod3, gate_idx, seq, tm=512, tn=512):
    per = seq // tm
    specs = [pl.BlockSpec((tm, tn), lambda j, i: (i, j)),
             pl.BlockSpec((None, 1, tn), lambda j, i: ((i // per) * 6 + gate_idx, 0, j))]
    return _mm_call(_mm_resid_kernel, a, [w], [0], [x2, mod3], specs, w.shape[1], F32, tm, tn, "mm_resid")


def _rope_tables(seq, rot_dim, pad_to):
    half = rot_dim // 2
    inv_freq = ROPE_THETA ** (-jnp.arange(0, rot_dim, 2, dtype=F32) / rot_dim)
    ang = jnp.arange(seq, dtype=F32)[:, None] * inv_freq[None, :]
    cos, sin = jnp.cos(ang), jnp.sin(ang)
    zero = jnp.zeros_like(sin)
    tail1 = jnp.ones((seq, pad_to - rot_dim), F32)
    tail0 = jnp.zeros((seq, pad_to - rot_dim), F32)
    if 2 * half == pad_to == LANES:
        return (jnp.concatenate([cos, cos], 1), jnp.concatenate([-sin, sin], 1),
                jnp.zeros((seq, pad_to), F32))
    t0 = jnp.concatenate([cos, cos, tail1], 1)
    t1 = jnp.concatenate([-sin, zero, tail0], 1)
    t2 = jnp.concatenate([zero, sin, tail0], 1)
    return t0, t1, t2


DSA_QB = 128
DSA_KB = 512


def _sortable_key(x):
    b = lax.bitcast_convert_type(x, I32)
    return b ^ ((b >> 31) & 0x7FFFFFFF)


def _dsa_kernel(q_ref, iq_ref, wt_ref, k_ref, vt_ref, ik_ref, o_ref,
                keys_ref, bias_ref, acc_ref, m_ref, l_ref, *, top_k):
    qb, kb_sz = DSA_QB, DSA_KB
    start = pl.program_id(1) * qb
    n_blk = (start + qb + kb_sz - 1) // kb_sz

    w_scaled = wt_ref[...] * (IDX_DIM ** -0.5 * IDX_HEADS ** -0.5)
    q_chunk = (start + lax.broadcasted_iota(I32, (kb_sz, qb), 1)) >> CHUNK_SHIFT
    k_iota = lax.broadcasted_iota(I32, (kb_sz, qb), 0)

    def score_blk(kb, carry):
        koff = pl.multiple_of(kb * kb_sz, kb_sz)
        ikb = ik_ref[pl.ds(koff, kb_sz), :]
        acc = jnp.zeros((kb_sz, qb), F32)
        for h in range(IDX_HEADS):
            r = lax.dot_general(ikb, iq_ref[:, h * IDX_DIM:(h + 1) * IDX_DIM], _NT,
                                preferred_element_type=F32)
            acc = acc + jnp.maximum(r, 0.0) * w_scaled[h:h + 1, :]
        visible = ((koff + k_iota) >> CHUNK_SHIFT) <= q_chunk
        keys_ref[pl.ds(koff, kb_sz), :] = _sortable_key(jnp.where(visible, acc, -jnp.inf))
        return carry

    lax.fori_loop(0, n_blk, score_blk, 0)

    def bit_step(b, thr):
        cand = thr + lax.shift_left(jnp.int32(1), 31 - b)

        def count_blk(kb, cnt):
            koff = pl.multiple_of(kb * kb_sz, kb_sz)
            ge = (keys_ref[pl.ds(koff, kb_sz), :] >= cand).astype(I32)
            return cnt + ge.reshape(kb_sz // SUBLANES, SUBLANES, qb).sum(axis=0)

        cnt = lax.fori_loop(0, n_blk, count_blk, jnp.zeros((SUBLANES, qb), I32))
        total = cnt.sum(axis=0, keepdims=True)
        return jnp.where(total >= top_k, cand, thr)

    thr = lax.fori_loop(0, 32, bit_step, jnp.full((1, qb), INT32_MIN, I32))

    def bias_blk(kb, carry):
        koff = pl.multiple_of(kb * kb_sz, kb_sz)
        blk = keys_ref[pl.ds(koff, kb_sz), :]
        bias_ref[pl.ds(koff, kb_sz), :] = jnp.where(
            blk >= thr, jnp.where(blk > NEG_INF_KEY, 0.0, NEG_BIG), NEG_BIG).astype(F32)
        return carry

    lax.fori_loop(0, n_blk, bias_blk, 0)

    m_ref[...] = jnp.full(m_ref.shape, NEG_BIG, F32)
    l_ref[...] = jnp.zeros(l_ref.shape, F32)
    acc_ref[...] = jnp.zeros(acc_ref.shape, F32)

    def att_blk(kb, carry):
        koff = pl.multiple_of(kb * kb_sz, kb_sz)
        bias = bias_ref[pl.ds(koff, kb_sz), :]
        for g in range(DSA_KV_HEADS):
            kblk = k_ref[pl.ds(koff, kb_sz), g * HEAD_DIM:(g + 1) * HEAD_DIM]
            v_t = vt_ref[kb, g * HEAD_DIM:(g + 1) * HEAD_DIM, :]
            for r in range(DSA_GROUP):
                h = g * DSA_GROUP + r
                s_t = lax.dot_general(kblk, q_ref[:, h * HEAD_DIM:(h + 1) * HEAD_DIM], _NT,
                                      preferred_element_type=F32) + bias
                m_prev = m_ref[h]
                m_new = jnp.maximum(m_prev, s_t.max(axis=0, keepdims=True))
                alpha = jnp.exp(m_prev - m_new)
                p = jnp.exp(s_t - m_new)
                l_ref[h] = alpha * l_ref[h] + p.sum(axis=0, keepdims=True)
                acc_ref[h] = alpha * acc_ref[h] + jnp.dot(v_t, p.astype(BF16),
                                                          preferred_element_type=F32)
                m_ref[h] = m_new
        return carry

    lax.fori_loop(0, n_blk, att_blk, 0)
    for h in range(DSA_HEADS):
        o_ref[:, h * HEAD_DIM:(h + 1) * HEAD_DIM] = (acc_ref[h] / l_ref[h]).T.astype(o_ref.dtype)


def _dsa_attention(qk, v_t, iq, ik, iw_t, batch, seq):
    m = qk.shape[0]
    qb, kb_sz = DSA_QB, DSA_KB
    assert seq % kb_sz == 0
    nq = seq // qb
    top_k = min(TOPK_MAX, seq // 4)
    d_q = DSA_HEADS * HEAD_DIM
    d_kv = DSA_KV_HEADS * HEAD_DIM
    return pl.pallas_call(
        functools.partial(_dsa_kernel, top_k=top_k),
        grid=(batch, nq),
        in_specs=[pl.BlockSpec((qb, d_q), lambda b, j: (b * nq + j, 0)),
                  pl.BlockSpec((qb, IDX_HEADS * IDX_DIM), lambda b, j: (b * nq + j, 0)),
                  pl.BlockSpec((IDX_HEADS, qb), lambda b, j: (b, j)),
                  pl.BlockSpec((seq, d_kv), lambda b, j: (b, d_q // d_kv)),
                  pl.BlockSpec((seq // kb_sz, d_kv, kb_sz), lambda b, j: (b, 0, 0)),
                  pl.BlockSpec((seq, IDX_DIM), lambda b, j: (b, 0))],
        out_specs=pl.BlockSpec((qb, d_q), lambda b, j: (b * nq + j, 0)),
        out_shape=jax.ShapeDtypeStruct((m, d_q), BF16),
        scratch_shapes=[pltpu.VMEM((seq, qb), I32),
                        pltpu.VMEM((seq, qb), F32),
                        pltpu.VMEM((DSA_HEADS, HEAD_DIM, qb), F32),
                        pltpu.VMEM((DSA_HEADS, 1, qb), F32),
                        pltpu.VMEM((DSA_HEADS, 1, qb), F32)],
        compiler_params=_params(2),
        name="dsa_attention",
    )(qk, iq, iw_t, qk, v_t, ik)


def _dsa_mixer(h, w_in, batch, seq):
    d_q = DSA_HEADS * HEAD_DIM
    d_kv = DSA_KV_HEADS * HEAD_DIM
    d_iq = IDX_HEADS * IDX_DIM
    off_v = d_q + d_kv
    off_iq = off_v + d_kv
    off_ik = off_iq + d_iq
    off_iw = off_ik + IDX_DIM
    w = w_in.astype(BF16)
    w_iw = jnp.pad(w[:, off_iw:], ((0, 0), (0, LANES - IDX_HEADS)))
    full_tabs = tuple(t * HEAD_DIM ** -0.25 for t in _rope_tables(seq, HEAD_DIM, HEAD_DIM))
    idx_tabs = _rope_tables(seq, IDX_ROPE_DIM, IDX_DIM)
    qk = _mm_rope(h, w, 0, d_q + d_kv, full_tabs, (True,), HEAD_DIM // 2, seq)
    v = _mm_plain(h, w, off_v, d_kv)
    iq = _mm_rope(h, w, off_iq, d_iq, idx_tabs, (True,), IDX_ROPE_DIM // 2, seq)
    ik = _mm_rope(h, w, off_ik, IDX_DIM, idx_tabs, (True,), IDX_ROPE_DIM // 2, seq)
    iw = _mm_plain(h, w_iw, 0, LANES, out_dtype=F32)
    iw_t = iw[:, :IDX_HEADS].reshape(batch, seq, IDX_HEADS).transpose(0, 2, 1)
    iw_t = iw_t.reshape(batch * IDX_HEADS, seq)
    v_t = v.reshape(batch * seq // DSA_KB, DSA_KB, d_kv).transpose(0, 2, 1)
    return _dsa_attention(qk, v_t, iq, ik, iw_t, batch, seq)


MLA_TQ = 256
MLA_TK = 512
MLA_HP = 4
MLA_QK_PAD = 2 * LANES
MLA_KV_W = MLA_NOPE_DIM + MLA_V_DIM


def _mla_kernel(q_ref, kv_ref, kr_ref, o_ref, acc_ref, m_ref, l_ref):
    tq, tk = MLA_TQ, MLA_TK
    start = pl.program_id(2) * tq
    n_blk = (start + tq + tk - 1) // tk
    n_full = (start + CHUNK) // tk
    q_chunk = (start + lax.broadcasted_iota(I32, (tq, tk), 0)) >> CHUNK_SHIFT
    k_iota = lax.broadcasted_iota(I32, (tq, tk), 1)
    m_ref[...] = jnp.full(m_ref.shape, NEG_BIG, F32)
    l_ref[...] = jnp.zeros(l_ref.shape, F32)
    acc_ref[...] = jnp.zeros(acc_ref.shape, F32)

    def blk(kb, carry, masked):
        koff = pl.multiple_of(kb * tk, tk)
        kr = kr_ref[pl.ds(koff, tk), :]
        if masked:
            visible = ((koff + k_iota) >> CHUNK_SHIFT) <= q_chunk
        for hh in range(MLA_HP):
            kn = kv_ref[pl.ds(koff, tk), hh * MLA_KV_W:hh * MLA_KV_W + MLA_NOPE_DIM]
            vb = kv_ref[pl.ds(koff, tk), hh * MLA_KV_W + MLA_NOPE_DIM:(hh + 1) * MLA_KV_W]
            q = q_ref[:, hh * MLA_QK_PAD:(hh + 1) * MLA_QK_PAD]
            s = lax.dot_general(q, jnp.concatenate([kn, kr], axis=1), _NT,
                                preferred_element_type=F32)
            if masked:
                s = jnp.where(visible, s, NEG_BIG)
            m_prev = m_ref[hh]
            m_new = jnp.maximum(m_prev, s.max(axis=1, keepdims=True))
            alpha = jnp.exp(m_prev - m_new)
            p = jnp.exp(s - m_new)
            l_ref[hh] = alpha * l_ref[hh] + p.sum(axis=1, keepdims=True)
            acc_ref[hh] = alpha * acc_ref[hh] + jnp.dot(p.astype(BF16), vb,
                                                        preferred_element_type=F32)
            m_ref[hh] = m_new
        return carry

    lax.fori_loop(0, n_full, functools.partial(blk, masked=False), 0)
    lax.fori_loop(n_full, n_blk, functools.partial(blk, masked=True), 0)
    for hh in range(MLA_HP):
        o_ref[:, hh * MLA_V_DIM:(hh + 1) * MLA_V_DIM] = (acc_ref[hh] / l_ref[hh]).astype(o_ref.dtype)


def _mla_attention(q, kv, k_rope, batch, seq):
    m = q.shape[0]
    tq, tk, hp = MLA_TQ, MLA_TK, MLA_HP
    assert seq % tk == 0 and seq % tq == 0 and MLA_HEADS % hp == 0
    nq = seq // tq
    return pl.pallas_call(
        _mla_kernel,
        grid=(batch, MLA_HEADS // hp, nq),
        in_specs=[pl.BlockSpec((tq, hp * MLA_QK_PAD), lambda b, h, i: (b * nq + i, h)),
                  pl.BlockSpec((seq, hp * MLA_KV_W), lambda b, h, i: (b, h)),
                  pl.BlockSpec((seq, LANES), lambda b, h, i: (b, 0))],
        out_specs=pl.BlockSpec((tq, hp * MLA_V_DIM), lambda b, h, i: (b * nq + i, h)),
        out_shape=jax.ShapeDtypeStruct((m, MLA_HEADS * MLA_V_DIM), BF16),
        scratch_shapes=[pltpu.VMEM((hp, tq, MLA_V_DIM), F32),
                        pltpu.VMEM((hp, tq, 1), F32),
                        pltpu.VMEM((hp, tq, 1), F32)],
        compiler_params=_params(3),
        name="mla_attention",
    )(q, kv, k_rope)


def _mla_mixer(h, w_down, q_norm, kv_norm, w_uq, w_ukv, batch, seq):
    wd = w_down.astype(BF16)
    wd_rope = jnp.pad(wd[:, MLA_Q_RANK + MLA_KV_RANK:], ((0, 0), (0, LANES - MLA_ROPE_DIM)))
    w_uq_h = w_uq.astype(BF16).reshape(MLA_Q_RANK, MLA_HEADS, MLA_NOPE_DIM + MLA_ROPE_DIM)
    pad = MLA_QK_PAD - MLA_NOPE_DIM - MLA_ROPE_DIM
    w_uq_p = jnp.pad(w_uq_h, ((0, 0), (0, 0), (0, pad))).reshape(MLA_Q_RANK, MLA_HEADS * MLA_QK_PAD)
    rope_tabs = _rope_tables(seq, MLA_ROPE_DIM, LANES)
    q_tabs = tuple(jnp.concatenate([jnp.zeros_like(t), t], axis=1) for t in rope_tabs)

    cq = _mm_rms(h, wd, 0, q_norm, out_scale=(MLA_NOPE_DIM + MLA_ROPE_DIM) ** -0.5)
    ckv = _mm_rms(h, wd, MLA_Q_RANK, kv_norm)
    k_rope = _mm_rope(h, wd_rope, 0, LANES, rope_tabs, (True,), MLA_ROPE_DIM // 2, seq)
    q = _mm_rope(cq, w_uq_p, 0, MLA_HEADS * MLA_QK_PAD, q_tabs, (False, True), MLA_ROPE_DIM // 2, seq)
    kv = _mm_plain(ckv, w_ukv.astype(BF16), 0, MLA_HEADS * (MLA_NOPE_DIM + MLA_V_DIM))
    return _mla_attention(q, kv, k_rope, batch, seq)


SB_T = 256
SB_HP = 4
SB_EXP_ZERO_BELOW = -104.0


def _sb_kernel(q_ref, k_ref, v_ref, o_ref, acc_ref, c_ref):
    t = SB_T
    i = pl.program_id(2)
    scale = HEAD_DIM ** -0.5
    row = lax.broadcasted_iota(I32, (t, t), 0)
    col = lax.broadcasted_iota(I32, (t, t), 1)
    later = (row > col).astype(BF16)
    acc_ref[...] = jnp.zeros(acc_ref.shape, F32)
    c_ref[...] = jnp.zeros(c_ref.shape, F32)

    def step(n, diagonal):
        koff = pl.multiple_of((i - n) * t, t)
        c_max = jnp.float32(-jnp.inf)
        for hh in range(SB_HP):
            lanes = slice(hh * HEAD_DIM, (hh + 1) * HEAD_DIM)
            z = lax.dot_general(q_ref[:, lanes], k_ref[pl.ds(koff, t), lanes], _NT,
                                preferred_element_type=F32) * scale
            sp = jnp.log1p(jnp.exp(-jnp.abs(z)))
            log_beta = jnp.minimum(z, 0.0) - sp
            log_keep = jnp.minimum(-z, 0.0) - sp
            if diagonal:
                before = col < row
                log_keep = jnp.where(before, log_keep, 0.0)
            hi = log_keep.astype(BF16)
            lo = (log_keep - hi.astype(F32)).astype(BF16)
            within = (jnp.dot(hi, later, preferred_element_type=F32)
                      + jnp.dot(lo, later, preferred_element_type=F32))
            c = c_ref[hh]
            a = jnp.exp(log_beta + within + c)
            if diagonal:
                a = jnp.where(before, a, 0.0)
            acc_ref[hh] += jnp.dot(a.astype(BF16), v_ref[pl.ds(koff, t), lanes],
                                   preferred_element_type=F32)
            c = c + log_keep.sum(axis=1, keepdims=True)
            c_ref[hh] = c
            c_max = jnp.maximum(c_max, jnp.max(c))
        return c_max

    def cond(state):
        n, c_max = state
        return jnp.logical_and(n <= i, c_max >= SB_EXP_ZERO_BELOW)

    def body(state):
        n, _ = state
        return n + 1, step(n, diagonal=False)

    lax.while_loop(cond, body, (jnp.int32(1), step(0, diagonal=True)))
    for hh in range(SB_HP):
        o_ref[:, hh * HEAD_DIM:(hh + 1) * HEAD_DIM] = acc_ref[hh].astype(o_ref.dtype)


def _sb_attention(qkv, batch, seq):
    m = qkv.shape[0]
    t, hp = SB_T, SB_HP
    assert seq % t == 0 and SB_HEADS % hp == 0
    nq = seq // t
    ng = SB_HEADS // hp
    w = hp * HEAD_DIM
    return pl.pallas_call(
        _sb_kernel,
        grid=(batch, ng, nq),
        in_specs=[pl.BlockSpec((t, w), lambda b, h, i: (b * nq + i, h)),
                  pl.BlockSpec((seq, w), lambda b, h, i: (b, ng + h)),
                  pl.BlockSpec((seq, w), lambda b, h, i: (b, 2 * ng + h))],
        out_specs=pl.BlockSpec((t, w), lambda b, h, i: (b * nq + i, h)),
        out_shape=jax.ShapeDtypeStruct((m, SB_HEADS * HEAD_DIM), BF16),
        scratch_shapes=[pltpu.VMEM((hp, t, HEAD_DIM), F32),
                        pltpu.VMEM((hp, t, 1), F32)],
        compiler_params=_params(3),
        name="sb_attention",
    )(qkv, qkv, qkv)


def _sb_mixer(h, w_in, batch, seq):
    qkv = _mm_plain(h, w_in.astype(BF16), 0, 3 * SB_HEADS * HEAD_DIM)
    return _sb_attention(qkv, batch, seq)


def _trunk(x, c, layers, final_norm):
    batch, seq, d = x.shape
    c_pad = jnp.pad(c, ((0, SUBLANES - batch), (0, 0)))
    x2 = x.reshape(batch * seq, d)
    for kind, ada_w, ada_b, g_mix, g_ffn, mix, w_ff_in, w_ff_out in layers:
        mod3 = _adaln(c_pad, ada_w, ada_b).reshape(SUBLANES * 6, 1, d)
        h = _norm_mod(x2, g_mix, mod3, 1, 0, seq)
        if kind == 0:
            o = _dsa_mixer(h, mix[0], batch, seq)
        elif kind == 1:
            o = _mla_mixer(h, *mix[:5], batch, seq)
        else:
            o = _sb_mixer(h, mix[0], batch, seq)
        x2 = _mm_resid(o, mix[-1].astype(BF16), x2, mod3, 2, seq)
        h = _norm_mod(x2, g_ffn, mod3, 4, 3, seq)
        d_ff = w_ff_out.shape[0]
        u = _mm_swiglu(h, w_ff_in.astype(BF16), d_ff)
        x2 = _mm_resid(u, w_ff_out.astype(BF16), x2, mod3, 5, seq)
    return _norm_final(x2, final_norm).reshape(batch, seq, d)


def kernel(x, c, ada_w_0, ada_b_0, norm_mix_0, norm_ffn_0, dsa_w_in_0, dsa_w_out_0, ffn_w_in_0, ffn_w_out_0, ada_w_1, ada_b_1, norm_mix_1, norm_ffn_1, mla_w_down_1, mla_q_norm_1, mla_kv_norm_1, mla_w_uq_1, mla_w_ukv_1, mla_w_out_1, ffn_w_in_1, ffn_w_out_1, ada_w_2, ada_b_2, norm_mix_2, norm_ffn_2, sb_w_in_2, sb_w_out_2, ffn_w_in_2, ffn_w_out_2, ada_w_3, ada_b_3, norm_mix_3, norm_ffn_3, dsa_w_in_3, dsa_w_out_3, ffn_w_in_3, ffn_w_out_3, final_norm):
    layers = (
        (0, ada_w_0, ada_b_0, norm_mix_0, norm_ffn_0, (dsa_w_in_0, dsa_w_out_0), ffn_w_in_0, ffn_w_out_0),
        (1, ada_w_1, ada_b_1, norm_mix_1, norm_ffn_1,
         (mla_w_down_1, mla_q_norm_1, mla_kv_norm_1, mla_w_uq_1, mla_w_ukv_1, mla_w_out_1),
         ffn_w_in_1, ffn_w_out_1),
        (2, ada_w_2, ada_b_2, norm_mix_2, norm_ffn_2, (sb_w_in_2, sb_w_out_2), ffn_w_in_2, ffn_w_out_2),
        (0, ada_w_3, ada_b_3, norm_mix_3, norm_ffn_3, (dsa_w_in_3, dsa_w_out_3), ffn_w_in_3, ffn_w_out_3),
    )
    return _trunk(x, c, layers, final_norm)
```

```python
import functools
import math

import numpy as np
import jax
import jax.numpy as jnp
from jax import lax
from jax.experimental import pallas as pl
from jax.experimental.pallas import tpu as pltpu

F32 = jnp.float32
BF16 = jnp.bfloat16
I32 = jnp.int32

V7X_VMEM_BYTES = 64 * 1024 * 1024
VMEM_LIMIT_BYTES = V7X_VMEM_BYTES * 7 // 8
LANES = 128
SUBLANES = 8

CHUNK = 64
CHUNK_SHIFT = int(math.log2(CHUNK))
assert 1 << CHUNK_SHIFT == CHUNK
ROPE_THETA = 10000.0
NORM_EPS = 1e-6
DSA_HEADS = 16
DSA_KV_HEADS = 4
DSA_GROUP = DSA_HEADS // DSA_KV_HEADS
HEAD_DIM = 128
IDX_HEADS = 16
IDX_DIM = 128
IDX_ROPE_DIM = 64
TOPK_MAX = 256
MLA_HEADS = 16
MLA_Q_RANK = 512
MLA_KV_RANK = 512
MLA_NOPE_DIM = 128
MLA_ROPE_DIM = 64
MLA_V_DIM = 128
SB_HEADS = 16

NEG_BIG = -1e30
INT32_MIN = -(2 ** 31)
NEG_INF_KEY = int(np.int32(np.uint32(0xFF800000) ^ np.uint32(0x7FFFFFFF)))

_NT = (((1,), (1,)), ((), ()))


def _params(n_grid_dims):
    return pltpu.CompilerParams(
        dimension_semantics=("arbitrary",) * n_grid_dims,
        vmem_limit_bytes=VMEM_LIMIT_BYTES)


def _sigmoid(x):
    return 1.0 / (1.0 + jnp.exp(-x))


def _adaln_kernel(c_ref, w_ref, b_ref, o_ref):
    c = c_ref[...]
    s = (c * _sigmoid(c)).astype(BF16)
    o_ref[...] = jnp.dot(s, w_ref[...].astype(BF16), preferred_element_type=F32) + b_ref[...]


def _adaln(c_pad, w, b):
    rows, d = c_pad.shape
    n = w.shape[1]
    tn = 1024
    return pl.pallas_call(
        _adaln_kernel,
        grid=(n // tn,),
        in_specs=[pl.BlockSpec((rows, d), lambda j: (0, 0)),
                  pl.BlockSpec((d, tn), lambda j: (0, j)),
                  pl.BlockSpec((1, tn), lambda j: (0, j))],
        out_specs=pl.BlockSpec((rows, tn), lambda j: (0, j)),
        out_shape=jax.ShapeDtypeStruct((rows, n), F32),
        compiler_params=_params(1),
        name="adaln",
    )(c_pad, w, b.reshape(1, n))


def _rms(x, g):
    return x * lax.rsqrt(jnp.mean(x * x, axis=-1, keepdims=True) + NORM_EPS) * g


def _norm_mod_kernel(x_ref, g_ref, sc_ref, sh_ref, o_ref):
    y = _rms(x_ref[...], g_ref[...])
    o_ref[...] = (y * (1.0 + sc_ref[...]) + sh_ref[...]).astype(o_ref.dtype)


def _norm_kernel(x_ref, g_ref, o_ref):
    o_ref[...] = _rms(x_ref[...], g_ref[...]).astype(o_ref.dtype)


def _norm_mod(x2, g, mod3, sc_idx, sh_idx, seq):
    m, d = x2.shape
    ts = 512
    per = seq // ts
    return pl.pallas_call(
        _norm_mod_kernel,
        grid=(m // ts,),
        in_specs=[pl.BlockSpec((ts, d), lambda i: (i, 0)),
                  pl.BlockSpec((1, d), lambda i: (0, 0)),
                  pl.BlockSpec((None, 1, d), lambda i: ((i // per) * 6 + sc_idx, 0, 0)),
                  pl.BlockSpec((None, 1, d), lambda i: ((i // per) * 6 + sh_idx, 0, 0))],
        out_specs=pl.BlockSpec((ts, d), lambda i: (i, 0)),
        out_shape=jax.ShapeDtypeStruct((m, d), BF16),
        compiler_params=_params(1),
        name="norm_mod",
    )(x2, g.reshape(1, d), mod3, mod3)


def _norm_final(x2, g):
    m, d = x2.shape
    ts = 512
    return pl.pallas_call(
        _norm_kernel,
        grid=(m // ts,),
        in_specs=[pl.BlockSpec((ts, d), lambda i: (i, 0)),
                  pl.BlockSpec((1, d), lambda i: (0, 0))],
        out_specs=pl.BlockSpec((ts, d), lambda i: (i, 0)),
        out_shape=jax.ShapeDtypeStruct((m, d), F32),
        compiler_params=_params(1),
        name="norm_final",
    )(x2, g.reshape(1, d))


def _dot(a_ref, w_ref, wb_ref):
    @pl.when(pl.program_id(1) == 0)
    def _():
        wb_ref[...] = w_ref[...].astype(BF16)

    return jnp.dot(a_ref[...], wb_ref[...], preferred_element_type=F32)


def _mm_plain_kernel(a_ref, w_ref, o_ref, wb_ref):
    o_ref[...] = _dot(a_ref, w_ref, wb_ref).astype(o_ref.dtype)


def _mm_rope_kernel(a_ref, w_ref, t0_ref, t1_ref, t2_ref, o_ref, wb_ref, *, kinds, shift):
    acc = _dot(a_ref, w_ref, wb_ref)
    period = len(kinds)
    for c in range(acc.shape[1] // LANES):
        y = acc[:, c * LANES:(c + 1) * LANES]
        k = c % period
        if kinds[k]:
            t = slice(k * LANES, (k + 1) * LANES)
            out = y * t0_ref[:, t] + pltpu.roll(y, LANES - shift, 1) * t1_ref[:, t]
            if 2 * shift != LANES:
                out = out + pltpu.roll(y, shift, 1) * t2_ref[:, t]
        else:
            out = y
        o_ref[:, c * LANES:(c + 1) * LANES] = out.astype(o_ref.dtype)


def _mm_rms_kernel(a_ref, w_ref, g_ref, o_ref, wb_ref, *, out_scale):
    o_ref[...] = (_rms(_dot(a_ref, w_ref, wb_ref), g_ref[...]) * out_scale).astype(o_ref.dtype)


def _mm_swiglu_kernel(a_ref, wg_ref, wu_ref, o_ref, wgb_ref, wub_ref):
    g = _dot(a_ref, wg_ref, wgb_ref)
    u = _dot(a_ref, wu_ref, wub_ref)
    o_ref[...] = (g * _sigmoid(g) * u).astype(o_ref.dtype)


def _mm_resid_kernel(a_ref, w_ref, x_ref, gt_ref, o_ref, wb_ref):
    o_ref[...] = x_ref[...] + gt_ref[...] * _dot(a_ref, w_ref, wb_ref)


def _mm_call(kernel, a, weights, w_offs, extras, extra_specs, n_out, out_dtype, tm, tn, name):
    m, k = a.shape
    assert m % tm == 0 and n_out % tn == 0
    in_specs = [pl.BlockSpec((tm, k), lambda j, i: (i, 0))]
    for off in w_offs:
        assert off % tn == 0
        in_specs.append(pl.BlockSpec((k, tn), lambda j, i, o=off // tn: (0, j + o)))
    in_specs += extra_specs
    return pl.pallas_call(
        kernel,
        grid=(n_out // tn, m // tm),
        in_specs=in_specs,
        out_specs=pl.BlockSpec((tm, tn), lambda j, i: (i, j)),
        out_shape=jax.ShapeDtypeStruct((m, n_out), out_dtype),
        scratch_shapes=[pltpu.VMEM((k, tn), BF16)] * len(weights),
        compiler_params=_params(2),
        name=name,
    )(a, *weights, *extras)


def _mm_plain(a, w, col_off, n_out, out_dtype=BF16, tm=1024, tn=512):
    tn = min(tn, n_out)
    return _mm_call(_mm_plain_kernel, a, [w], [col_off], [], [], n_out, out_dtype, tm, tn, "mm_plain")


def _mm_rope(a, w, col_off, n_out, tables, kinds, shift, seq, tm=1024, tn=512):
    tn = min(tn, n_out)
    per = seq // tm
    width = tables[0].shape[1]
    assert width == LANES * len(kinds) and tn % width == 0
    specs = [pl.BlockSpec((tm, width), lambda j, i: (i % per, 0))] * 3
    kern = functools.partial(_mm_rope_kernel, kinds=kinds, shift=shift)
    return _mm_call(kern, a, [w], [col_off], list(tables), specs, n_out, BF16, tm, tn, "mm_rope")


def _mm_rms(a, w, col_off, g, out_scale=1.0, tm=1024):
    n = g.shape[0]
    specs = [pl.BlockSpec((1, n), lambda j, i: (0, 0))]
    kern = functools.partial(_mm_rms_kernel, out_scale=out_scale)
    return _mm_call(kern, a, [w], [col_off], [g.reshape(1, n)], specs, n, BF16, tm, n, "mm_rms")


def _mm_swiglu(a, w, d_ff, tm=1024, tn=512):
    return _mm_call(_mm_swiglu_kernel, a, [w, w], [0, d_ff], [], [], d_ff, BF16, tm, tn, "mm_swiglu")


def _mm_resid(a, w, x2, mod3, gate_idx, seq, tm=512, tn=512):
    per = seq // tm
    specs = [pl.BlockSpec((tm, tn), lambda j, i: (i, j)),
             pl.BlockSpec((None, 1, tn), lambda j, i: ((i // per) * 6 + gate_idx, 0, j))]
    return _mm_call(_mm_resid_kernel, a, [w], [0], [x2, mod3], specs, w.shape[1], F32, tm, tn, "mm_resid")


def _rope_tables(seq, rot_dim, pad_to):
    half = rot_dim // 2
    inv_freq = ROPE_THETA ** (-jnp.arange(0, rot_dim, 2, dtype=F32) / rot_dim)
    ang = jnp.arange(seq, dtype=F32)[:, None] * inv_freq[None, :]
    cos, sin = jnp.cos(ang), jnp.sin(ang)
    zero = jnp.zeros_like(sin)
    tail1 = jnp.ones((seq, pad_to - rot_dim), F32)
    tail0 = jnp.zeros((seq, pad_to - rot_dim), F32)
    if 2 * half == pad_to == LANES:
        return (jnp.concatenate([cos, cos], 1), jnp.concatenate([-sin, sin], 1),
                jnp.zeros((seq, pad_to), F32))
    t0 = jnp.concatenate([cos, cos, tail1], 1)
    t1 = jnp.concatenate([-sin, zero, tail0], 1)
    t2 = jnp.concatenate([zero, sin, tail0], 1)
    return t0, t1, t2


DSA_QB = 128
DSA_KB = 512


def _sortable_key(x):
    b = lax.bitcast_convert_type(x, I32)
    return b ^ ((b >> 31) & 0x7FFFFFFF)


def _dsa_kernel(q_ref, iq_ref, wt_ref, k_ref, vt_ref, ik_ref, o_ref,
                keys_ref, bias_ref, acc_ref, m_ref, l_ref, *, top_k):
    qb, kb_sz = DSA_QB, DSA_KB
    start = pl.program_id(1) * qb
    n_blk = (start + qb + kb_sz - 1) // kb_sz

    w_scaled = wt_ref[...] * (IDX_DIM ** -0.5 * IDX_HEADS ** -0.5)
    q_chunk = (start + lax.broadcasted_iota(I32, (kb_sz, qb), 1)) >> CHUNK_SHIFT
    k_iota = lax.broadcasted_iota(I32, (kb_sz, qb), 0)

    def score_blk(kb, carry):
        koff = pl.multiple_of(kb * kb_sz, kb_sz)
        ikb = ik_ref[pl.ds(koff, kb_sz), :]
        acc = jnp.zeros((kb_sz, qb), F32)
        for h in range(IDX_HEADS):
            r = lax.dot_general(ikb, iq_ref[:, h * IDX_DIM:(h + 1) * IDX_DIM], _NT,
                                preferred_element_type=F32)
            acc = acc + jnp.maximum(r, 0.0) * w_scaled[h:h + 1, :]
        visible = ((koff + k_iota) >> CHUNK_SHIFT) <= q_chunk
        keys_ref[pl.ds(koff, kb_sz), :] = _sortable_key(jnp.where(visible, acc, -jnp.inf))
        return carry

    lax.fori_loop(0, n_blk, score_blk, 0)

    def bit_step(b, thr):
        cand = thr + lax.shift_left(jnp.int32(1), 31 - b)

        def count_blk(kb, cnt):
            koff = pl.multiple_of(kb * kb_sz, kb_sz)
            ge = (keys_ref[pl.ds(koff, kb_sz), :] >= cand).astype(I32)
            return cnt + ge.reshape(kb_sz // SUBLANES, SUBLANES, qb).sum(axis=0)

        cnt = lax.fori_loop(0, n_blk, count_blk, jnp.zeros((SUBLANES, qb), I32))
        total = cnt.sum(axis=0, keepdims=True)
        return jnp.where(total >= top_k, cand, thr)

    thr = lax.fori_loop(0, 32, bit_step, jnp.full((1, qb), INT32_MIN, I32))

    def bias_blk(kb, carry):
        koff = pl.multiple_of(kb * kb_sz, kb_sz)
        blk = keys_ref[pl.ds(koff, kb_sz), :]
        bias_ref[pl.ds(koff, kb_sz), :] = jnp.where(
            blk >= thr, jnp.where(blk > NEG_INF_KEY, 0.0, NEG_BIG), NEG_BIG).astype(F32)
        return carry

    lax.fori_loop(0, n_blk, bias_blk, 0)

    m_ref[...] = jnp.full(m_ref.shape, NEG_BIG, F32)
    l_ref[...] = jnp.zeros(l_ref.shape, F32)
    acc_ref[...] = jnp.zeros(acc_ref.shape, F32)

    def att_blk(kb, carry):
        koff = pl.multiple_of(kb * kb_sz, kb_sz)
        bias = bias_ref[pl.ds(koff, kb_sz), :]
        for g in range(DSA_KV_HEADS):
            kblk = k_ref[pl.ds(koff, kb_sz), g * HEAD_DIM:(g + 1) * HEAD_DIM]
            v_t = vt_ref[kb, g * HEAD_DIM:(g + 1) * HEAD_DIM, :]
            for r in range(DSA_GROUP):
                h = g * DSA_GROUP + r
                s_t = lax.dot_general(kblk, q_ref[:, h * HEAD_DIM:(h + 1) * HEAD_DIM], _NT,
                                      preferred_element_type=F32) + bias
                m_prev = m_ref[h]
                m_new = jnp.maximum(m_prev, s_t.max(axis=0, keepdims=True))
                alpha = jnp.exp(m_prev - m_new)
                p = jnp.exp(s_t - m_new)
                l_ref[h] = alpha * l_ref[h] + p.sum(axis=0, keepdims=True)
                acc_ref[h] = alpha * acc_ref[h] + jnp.dot(v_t, p.astype(BF16),
                                                          preferred_element_type=F32)
                m_ref[h] = m_new
        return carry

    lax.fori_loop(0, n_blk, att_blk, 0)
    for h in range(DSA_HEADS):
        o_ref[:, h * HEAD_DIM:(h + 1) * HEAD_DIM] = (acc_ref[h] / l_ref[h]).T.astype(o_ref.dtype)


def _dsa_attention(qk, v_t, iq, ik, iw_t, batch, seq):
    m = qk.shape[0]
    qb, kb_sz = DSA_QB, DSA_KB
    assert seq % kb_sz == 0
    nq = seq // qb
    top_k = min(TOPK_MAX, seq // 4)
    d_q = DSA_HEADS * HEAD_DIM
    d_kv = DSA_KV_HEADS * HEAD_DIM
    return pl.pallas_call(
        functools.partial(_dsa_kernel, top_k=top_k),
        grid=(batch, nq),
        in_specs=[pl.BlockSpec((qb, d_q), lambda b, j: (b * nq + j, 0)),
                  pl.BlockSpec((qb, IDX_HEADS * IDX_DIM), lambda b, j: (b * nq + j, 0)),
                  pl.BlockSpec((IDX_HEADS, qb), lambda b, j: (b, j)),
                  pl.BlockSpec((seq, d_kv), lambda b, j: (b, d_q // d_kv)),
                  pl.BlockSpec((seq // kb_sz, d_kv, kb_sz), lambda b, j: (b, 0, 0)),
                  pl.BlockSpec((seq, IDX_DIM), lambda b, j: (b, 0))],
        out_specs=pl.BlockSpec((qb, d_q), lambda b, j: (b * nq + j, 0)),
        out_shape=jax.ShapeDtypeStruct((m, d_q), BF16),
        scratch_shapes=[pltpu.VMEM((seq, qb), I32),
                        pltpu.VMEM((seq, qb), F32),
                        pltpu.VMEM((DSA_HEADS, HEAD_DIM, qb), F32),
                        pltpu.VMEM((DSA_HEADS, 1, qb), F32),
                        pltpu.VMEM((DSA_HEADS, 1, qb), F32)],
        compiler_params=_params(2),
        name="dsa_attention",
    )(qk, iq, iw_t, qk, v_t, ik)


def _dsa_mixer(h, w_in, batch, seq):
    d_q = DSA_HEADS * HEAD_DIM
    d_kv = DSA_KV_HEADS * HEAD_DIM
    d_iq = IDX_HEADS * IDX_DIM
    off_v = d_q + d_kv
    off_iq = off_v + d_kv
    off_ik = off_iq + d_iq
    off_iw = off_ik + IDX_DIM
    w = w_in
    w_iw = jnp.pad(w[:, off_iw:], ((0, 0), (0, LANES - IDX_HEADS)))
    full_tabs = tuple(t * HEAD_DIM ** -0.25 for t in _rope_tables(seq, HEAD_DIM, HEAD_DIM))
    idx_tabs = _rope_tables(seq, IDX_ROPE_DIM, IDX_DIM)
    qk = _mm_rope(h, w, 0, d_q + d_kv, full_tabs, (True,), HEAD_DIM // 2, seq)
    v = _mm_plain(h, w, off_v, d_kv)
    iq = _mm_rope(h, w, off_iq, d_iq, idx_tabs, (True,), IDX_ROPE_DIM // 2, seq)
    ik = _mm_rope(h, w, off_ik, IDX_DIM, idx_tabs, (True,), IDX_ROPE_DIM // 2, seq)
    iw = _mm_plain(h, w_iw, 0, LANES, out_dtype=F32)
    iw_t = iw[:, :IDX_HEADS].reshape(batch, seq, IDX_HEADS).transpose(0, 2, 1)
    iw_t = iw_t.reshape(batch * IDX_HEADS, seq)
    v_t = v.reshape(batch * seq // DSA_KB, DSA_KB, d_kv).transpose(0, 2, 1)
    return _dsa_attention(qk, v_t, iq, ik, iw_t, batch, seq)


MLA_TQ = 256
MLA_TK = 512
MLA_HP = 4
MLA_QK_PAD = 2 * LANES
MLA_KV_W = MLA_NOPE_DIM + MLA_V_DIM


def _mla_kernel(q_ref, kv_ref, kr_ref, o_ref, acc_ref, m_ref, l_ref):
    tq, tk = MLA_TQ, MLA_TK
    start = pl.program_id(2) * tq
    n_blk = (start + tq + tk - 1) // tk
    n_full = (start + CHUNK) // tk
    q_chunk = (start + lax.broadcasted_iota(I32, (tq, tk), 0)) >> CHUNK_SHIFT
    k_iota = lax.broadcasted_iota(I32, (tq, tk), 1)
    m_ref[...] = jnp.full(m_ref.shape, NEG_BIG, F32)
    l_ref[...] = jnp.zeros(l_ref.shape, F32)
    acc_ref[...] = jnp.zeros(acc_ref.shape, F32)

    def blk(kb, carry, masked):
        koff = pl.multiple_of(kb * tk, tk)
        kr = kr_ref[pl.ds(koff, tk), :]
        if masked:
            visible = ((koff + k_iota) >> CHUNK_SHIFT) <= q_chunk
        for hh in range(MLA_HP):
            kn = kv_ref[pl.ds(koff, tk), hh * MLA_KV_W:hh * MLA_KV_W + MLA_NOPE_DIM]
            vb = kv_ref[pl.ds(koff, tk), hh * MLA_KV_W + MLA_NOPE_DIM:(hh + 1) * MLA_KV_W]
            q = q_ref[:, hh * MLA_QK_PAD:(hh + 1) * MLA_QK_PAD]
            s = lax.dot_general(q, jnp.concatenate([kn, kr], axis=1), _NT,
                                preferred_element_type=F32)
            if masked:
                s = jnp.where(visible, s, NEG_BIG)
            m_prev = m_ref[hh]
            m_new = jnp.maximum(m_prev, s.max(axis=1, keepdims=True))
            alpha = jnp.exp(m_prev - m_new)
            p = jnp.exp(s - m_new)
            l_ref[hh] = alpha * l_ref[hh] + p.sum(axis=1, keepdims=True)
            acc_ref[hh] = alpha * acc_ref[hh] + jnp.dot(p.astype(BF16), vb,
                                                        preferred_element_type=F32)
            m_ref[hh] = m_new
        return carry

    lax.fori_loop(0, n_full, functools.partial(blk, masked=False), 0)
    lax.fori_loop(n_full, n_blk, functools.partial(blk, masked=True), 0)
    for hh in range(MLA_HP):
        o_ref[:, hh * MLA_V_DIM:(hh + 1) * MLA_V_DIM] = (acc_ref[hh] / l_ref[hh]).astype(o_ref.dtype)


def _mla_attention(q, kv, k_rope, batch, seq):
    m = q.shape[0]
    tq, tk, hp = MLA_TQ, MLA_TK, MLA_HP
    assert seq % tk == 0 and seq % tq == 0 and MLA_HEADS % hp == 0
    nq = seq // tq
    return pl.pallas_call(
        _mla_kernel,
        grid=(batch, MLA_HEADS // hp, nq),
        in_specs=[pl.BlockSpec((tq, hp * MLA_QK_PAD), lambda b, h, i: (b * nq + i, h)),
                  pl.BlockSpec((seq, hp * MLA_KV_W), lambda b, h, i: (b, h)),
                  pl.BlockSpec((seq, LANES), lambda b, h, i: (b, 0))],
        out_specs=pl.BlockSpec((tq, hp * MLA_V_DIM), lambda b, h, i: (b * nq + i, h)),
        out_shape=jax.ShapeDtypeStruct((m, MLA_HEADS * MLA_V_DIM), BF16),
        scratch_shapes=[pltpu.VMEM((hp, tq, MLA_V_DIM), F32),
                        pltpu.VMEM((hp, tq, 1), F32),
                        pltpu.VMEM((hp, tq, 1), F32)],
        compiler_params=_params(3),
        name="mla_attention",
    )(q, kv, k_rope)


def _mla_mixer(h, w_down, q_norm, kv_norm, w_uq, w_ukv, batch, seq):
    wd = w_down
    wd_rope = jnp.pad(wd[:, MLA_Q_RANK + MLA_KV_RANK:], ((0, 0), (0, LANES - MLA_ROPE_DIM)))
    w_uq_h = w_uq.reshape(MLA_Q_RANK, MLA_HEADS, MLA_NOPE_DIM + MLA_ROPE_DIM)
    pad = MLA_QK_PAD - MLA_NOPE_DIM - MLA_ROPE_DIM
    w_uq_p = jnp.pad(w_uq_h, ((0, 0), (0, 0), (0, pad))).reshape(MLA_Q_RANK, MLA_HEADS * MLA_QK_PAD)
    rope_tabs = _rope_tables(seq, MLA_ROPE_DIM, LANES)
    q_tabs = tuple(jnp.concatenate([jnp.zeros_like(t), t], axis=1) for t in rope_tabs)

    cq = _mm_rms(h, wd, 0, q_norm, out_scale=(MLA_NOPE_DIM + MLA_ROPE_DIM) ** -0.5)
    ckv = _mm_rms(h, wd, MLA_Q_RANK, kv_norm)
    k_rope = _mm_rope(h, wd_rope, 0, LANES, rope_tabs, (True,), MLA_ROPE_DIM // 2, seq)
    q = _mm_rope(cq, w_uq_p, 0, MLA_HEADS * MLA_QK_PAD, q_tabs, (False, True), MLA_ROPE_DIM // 2, seq)
    kv = _mm_plain(ckv, w_ukv, 0, MLA_HEADS * (MLA_NOPE_DIM + MLA_V_DIM))
    return _mla_attention(q, kv, k_rope, batch, seq)


SB_T = 256
SB_SUB = 256
SB_HP = 4
SB_EXP_ZERO_BELOW = -104.0


def _sb_kernel(q_ref, k_ref, v_ref, o_ref, acc_ref, c_ref):
    t, sub = SB_T, SB_SUB
    n_sub = t // sub
    i = pl.program_id(2)
    scale = HEAD_DIM ** -0.5
    row = lax.broadcasted_iota(I32, (t, t), 0)
    col = lax.broadcasted_iota(I32, (t, t), 1)
    later = (row > col).astype(BF16)
    q_i = lax.broadcasted_iota(I32, (sub, t), 0)
    k_i = lax.broadcasted_iota(I32, (sub, t), 1)
    acc_ref[...] = jnp.zeros(acc_ref.shape, F32)
    c_ref[...] = jnp.zeros(c_ref.shape, F32)

    def step(n, diagonal):
        koff = pl.multiple_of((i - n) * t, t)
        c_max = jnp.float32(-jnp.inf)
        for hh in range(SB_HP):
            lanes = slice(hh * HEAD_DIM, (hh + 1) * HEAD_DIM)
            kblk = k_ref[pl.ds(koff, t), lanes]
            vblk = v_ref[pl.ds(koff, t), lanes]
            for cc in range(n_sub):
                idx = hh * n_sub + cc
                z = lax.dot_general(q_ref[cc * sub:(cc + 1) * sub, lanes], kblk, _NT,
                                    preferred_element_type=F32) * scale
                sp = jnp.log(1.0 + jnp.exp(-jnp.abs(z)))
                log_beta = jnp.minimum(z, 0.0) - sp
                log_keep = jnp.minimum(-z, 0.0) - sp
                if diagonal:
                    before = k_i < cc * sub + q_i
                    log_keep = jnp.where(before, log_keep, 0.0)
                hi = log_keep.astype(BF16)
                lo = (log_keep - hi.astype(F32)).astype(BF16)
                within = (jnp.dot(hi, later, preferred_element_type=F32)
                          + jnp.dot(lo, later, preferred_element_type=F32))
                c = c_ref[idx]
                a = jnp.exp(log_beta + within + c)
                if diagonal:
                    a = jnp.where(before, a, 0.0)
                acc_ref[idx] += jnp.dot(a.astype(BF16), vblk, preferred_element_type=F32)
                c = c + log_keep.sum(axis=1, keepdims=True)
                c_ref[idx] = c
                c_max = jnp.maximum(c_max, jnp.max(c))
        return c_max

    def cond(state):
        n, c_max = state
        return jnp.logical_and(n <= i, c_max >= SB_EXP_ZERO_BELOW)

    def body(state):
        n, _ = state
        return n + 1, step(n, diagonal=False)

    lax.while_loop(cond, body, (jnp.int32(1), step(0, diagonal=True)))
    for hh in range(SB_HP):
        for cc in range(n_sub):
            o_ref[cc * sub:(cc + 1) * sub, hh * HEAD_DIM:(hh + 1) * HEAD_DIM] = (
                acc_ref[hh * n_sub + cc].astype(o_ref.dtype))


def _sb_attention(qkv, batch, seq):
    m = qkv.shape[0]
    t, hp = SB_T, SB_HP
    assert seq % t == 0 and SB_HEADS % hp == 0
    nq = seq // t
    ng = SB_HEADS // hp
    w = hp * HEAD_DIM
    return pl.pallas_call(
        _sb_kernel,
        grid=(batch, ng, nq),
        in_specs=[pl.BlockSpec((t, w), lambda b, h, i: (b * nq + i, h)),
                  pl.BlockSpec((seq, w), lambda b, h, i: (b, ng + h)),
                  pl.BlockSpec((seq, w), lambda b, h, i: (b, 2 * ng + h))],
        out_specs=pl.BlockSpec((t, w), lambda b, h, i: (b * nq + i, h)),
        out_shape=jax.ShapeDtypeStruct((m, SB_HEADS * HEAD_DIM), BF16),
        scratch_shapes=[pltpu.VMEM((hp * t // SB_SUB, SB_SUB, HEAD_DIM), F32),
                        pltpu.VMEM((hp * t // SB_SUB, SB_SUB, 1), F32)],
        compiler_params=_params(3),
        name="sb_attention",
    )(qkv, qkv, qkv)


def _sb_mixer(h, w_in, batch, seq):
    qkv = _mm_plain(h, w_in, 0, 3 * SB_HEADS * HEAD_DIM)
    return _sb_attention(qkv, batch, seq)


def _trunk(x, c, layers, final_norm):
    batch, seq, d = x.shape
    c_pad = jnp.pad(c, ((0, SUBLANES - batch), (0, 0)))
    x2 = x.reshape(batch * seq, d)
    for kind, ada_w, ada_b, g_mix, g_ffn, mix, w_ff_in, w_ff_out in layers:
        mod3 = _adaln(c_pad, ada_w, ada_b).reshape(SUBLANES * 6, 1, d)
        h = _norm_mod(x2, g_mix, mod3, 1, 0, seq)
        if kind == 0:
            o = _dsa_mixer(h, mix[0], batch, seq)
        elif kind == 1:
            o = _mla_mixer(h, *mix[:5], batch, seq)
        else:
            o = _sb_mixer(h, mix[0], batch, seq)
        x2 = _mm_resid(o, mix[-1], x2, mod3, 2, seq)
        h = _norm_mod(x2, g_ffn, mod3, 4, 3, seq)
        d_ff = w_ff_out.shape[0]
        u = _mm_swiglu(h, w_ff_in, d_ff)
        x2 = _mm_resid(u, w_ff_out, x2, mod3, 5, seq)
    return _norm_final(x2, final_norm).reshape(batch, seq, d)


def kernel(x, c, ada_w_0, ada_b_0, norm_mix_0, norm_ffn_0, dsa_w_in_0, dsa_w_out_0, ffn_w_in_0, ffn_w_out_0, ada_w_1, ada_b_1, norm_mix_1, norm_ffn_1, mla_w_down_1, mla_q_norm_1, mla_kv_norm_1, mla_w_uq_1, mla_w_ukv_1, mla_w_out_1, ffn_w_in_1, ffn_w_out_1, ada_w_2, ada_b_2, norm_mix_2, norm_ffn_2, sb_w_in_2, sb_w_out_2, ffn_w_in_2, ffn_w_out_2, ada_w_3, ada_b_3, norm_mix_3, norm_ffn_3, dsa_w_in_3, dsa_w_out_3, ffn_w_in_3, ffn_w_out_3, final_norm):
    layers = (
        (0, ada_w_0, ada_b_0, norm_mix_0, norm_ffn_0, (dsa_w_in_0, dsa_w_out_0), ffn_w_in_0, ffn_w_out_0),
        (1, ada_w_1, ada_b_1, norm_mix_1, norm_ffn_1,
         (mla_w_down_1, mla_q_norm_1, mla_kv_norm_1, mla_w_uq_1, mla_w_ukv_1, mla_w_out_1),
         ffn_w_in_1, ffn_w_out_1),
        (2, ada_w_2, ada_b_2, norm_mix_2, norm_ffn_2, (sb_w_in_2, sb_w_out_2), ffn_w_in_2, ffn_w_out_2),
        (0, ada_w_3, ada_b_3, norm_mix_3, norm_ffn_3, (dsa_w_in_3, dsa_w_out_3), ffn_w_in_3, ffn_w_out_3),
    )
    return _trunk(x, c, layers, final_norm)
```

```python
import functools
import math

import numpy as np
import jax
import jax.numpy as jnp
from jax import lax
from jax.experimental import pallas as pl
from jax.experimental.pallas import tpu as pltpu

F32 = jnp.float32
BF16 = jnp.bfloat16
I32 = jnp.int32

V7X_VMEM_BYTES = 64 * 1024 * 1024
VMEM_LIMIT_BYTES = V7X_VMEM_BYTES * 7 // 8
LANES = 128
SUBLANES = 8

CHUNK = 64
CHUNK_SHIFT = int(math.log2(CHUNK))
assert 1 << CHUNK_SHIFT == CHUNK
ROPE_THETA = 10000.0
NORM_EPS = 1e-6
DSA_HEADS = 16
DSA_KV_HEADS = 4
DSA_GROUP = DSA_HEADS // DSA_KV_HEADS
HEAD_DIM = 128
IDX_HEADS = 16
IDX_DIM = 128
IDX_ROPE_DIM = 64
TOPK_MAX = 256
MLA_HEADS = 16
MLA_Q_RANK = 512
MLA_KV_RANK = 512
MLA_NOPE_DIM = 128
MLA_ROPE_DIM = 64
MLA_V_DIM = 128
SB_HEADS = 16

NEG_BIG = -1e30
INT32_MIN = -(2 ** 31)
NEG_INF_KEY = int(np.int32(np.uint32(0xFF800000) ^ np.uint32(0x7FFFFFFF)))

_NT = (((1,), (1,)), ((), ()))


def _params(n_grid_dims):
    return pltpu.CompilerParams(
        dimension_semantics=("arbitrary",) * n_grid_dims,
        vmem_limit_bytes=VMEM_LIMIT_BYTES)


def _sigmoid(x):
    return 1.0 / (1.0 + jnp.exp(-x))


def _adaln_kernel(c_ref, w_ref, b_ref, o_ref):
    c = c_ref[...]
    s = (c * _sigmoid(c)).astype(BF16)
    o_ref[...] = jnp.dot(s, w_ref[...].astype(BF16), preferred_element_type=F32) + b_ref[...]


def _adaln(c_pad, w, b):
    rows, d = c_pad.shape
    n = w.shape[1]
    tn = 1024
    return pl.pallas_call(
        _adaln_kernel,
        grid=(n // tn,),
        in_specs=[pl.BlockSpec((rows, d), lambda j: (0, 0)),
                  pl.BlockSpec((d, tn), lambda j: (0, j)),
                  pl.BlockSpec((1, tn), lambda j: (0, j))],
        out_specs=pl.BlockSpec((rows, tn), lambda j: (0, j)),
        out_shape=jax.ShapeDtypeStruct((rows, n), F32),
        compiler_params=_params(1),
        name="adaln",
    )(c_pad, w, b.reshape(1, n))


def _rms(x, g):
    return x * lax.rsqrt(jnp.mean(x * x, axis=-1, keepdims=True) + NORM_EPS) * g


def _norm_mod_kernel(x_ref, g_ref, sc_ref, sh_ref, o_ref):
    y = _rms(x_ref[...], g_ref[...])
    o_ref[...] = (y * (1.0 + sc_ref[...]) + sh_ref[...]).astype(o_ref.dtype)


def _norm_kernel(x_ref, g_ref, o_ref):
    o_ref[...] = _rms(x_ref[...], g_ref[...]).astype(o_ref.dtype)


def _norm_mod(x2, g, mod3, sc_idx, sh_idx, seq):
    m, d = x2.shape
    ts = 512
    per = seq // ts
    return pl.pallas_call(
        _norm_mod_kernel,
        grid=(m // ts,),
        in_specs=[pl.BlockSpec((ts, d), lambda i: (i, 0)),
                  pl.BlockSpec((1, d), lambda i: (0, 0)),
                  pl.BlockSpec((None, 1, d), lambda i: ((i // per) * 6 + sc_idx, 0, 0)),
                  pl.BlockSpec((None, 1, d), lambda i: ((i // per) * 6 + sh_idx, 0, 0))],
        out_specs=pl.BlockSpec((ts, d), lambda i: (i, 0)),
        out_shape=jax.ShapeDtypeStruct((m, d), BF16),
        compiler_params=_params(1),
        name="norm_mod",
    )(x2, g.reshape(1, d), mod3, mod3)


def _norm_final(x2, g):
    m, d = x2.shape
    ts = 512
    return pl.pallas_call(
        _norm_kernel,
        grid=(m // ts,),
        in_specs=[pl.BlockSpec((ts, d), lambda i: (i, 0)),
                  pl.BlockSpec((1, d), lambda i: (0, 0))],
        out_specs=pl.BlockSpec((ts, d), lambda i: (i, 0)),
        out_shape=jax.ShapeDtypeStruct((m, d), F32),
        compiler_params=_params(1),
        name="norm_final",
    )(x2, g.reshape(1, d))


def _dot(a_ref, w_ref, wb_ref):
    @pl.when(pl.program_id(1) == 0)
    def _():
        wb_ref[...] = w_ref[...].astype(BF16)

    return jnp.dot(a_ref[...], wb_ref[...], preferred_element_type=F32)


def _mm_plain_kernel(a_ref, w_ref, o_ref, wb_ref):
    o_ref[...] = _dot(a_ref, w_ref, wb_ref).astype(o_ref.dtype)


def _mm_rope_kernel(a_ref, w_ref, t0_ref, t1_ref, t2_ref, o_ref, wb_ref, *, kinds, shift):
    acc = _dot(a_ref, w_ref, wb_ref)
    period = len(kinds)
    for c in range(acc.shape[1] // LANES):
        y = acc[:, c * LANES:(c + 1) * LANES]
        k = c % period
        if kinds[k]:
            t = slice(k * LANES, (k + 1) * LANES)
            out = y * t0_ref[:, t] + pltpu.roll(y, LANES - shift, 1) * t1_ref[:, t]
            if 2 * shift != LANES:
                out = out + pltpu.roll(y, shift, 1) * t2_ref[:, t]
        else:
            out = y
        o_ref[:, c * LANES:(c + 1) * LANES] = out.astype(o_ref.dtype)


def _mm_rms_kernel(a_ref, w_ref, g_ref, o_ref, wb_ref, *, out_scale):
    o_ref[...] = (_rms(_dot(a_ref, w_ref, wb_ref), g_ref[...]) * out_scale).astype(o_ref.dtype)


def _mm_swiglu_kernel(a_ref, wg_ref, wu_ref, o_ref, wgb_ref, wub_ref):
    g = _dot(a_ref, wg_ref, wgb_ref)
    u = _dot(a_ref, wu_ref, wub_ref)
    o_ref[...] = (g * _sigmoid(g) * u).astype(o_ref.dtype)


def _mm_resid_kernel(a_ref, w_ref, x_ref, gt_ref, o_ref, wb_ref):
    o_ref[...] = x_ref[...] + gt_ref[...] * _dot(a_ref, w_ref, wb_ref)


def _mm_call(kernel, a, weights, w_offs, extras, extra_specs, n_out, out_dtype, tm, tn, name):
    m, k = a.shape
    assert m % tm == 0 and n_out % tn == 0
    in_specs = [pl.BlockSpec((tm, k), lambda j, i: (i, 0))]
    for off in w_offs:
        assert off % tn == 0
        in_specs.append(pl.BlockSpec((k, tn), lambda j, i, o=off // tn: (0, j + o)))
    in_specs += extra_specs
    return pl.pallas_call(
        kernel,
        grid=(n_out // tn, m // tm),
        in_specs=in_specs,
        out_specs=pl.BlockSpec((tm, tn), lambda j, i: (i, j)),
        out_shape=jax.ShapeDtypeStruct((m, n_out), out_dtype),
        scratch_shapes=[pltpu.VMEM((k, tn), BF16)] * len(weights),
        compiler_params=_params(2),
        name=name,
    )(a, *weights, *extras)


def _mm_plain(a, w, col_off, n_out, out_dtype=BF16, tm=1024, tn=512):
    tn = min(tn, n_out)
    return _mm_call(_mm_plain_kernel, a, [w], [col_off], [], [], n_out, out_dtype, tm, tn, "mm_plain")


def _mm_rope(a, w, col_off, n_out, tables, kinds, shift, seq, tm=1024, tn=512):
    tn = min(tn, n_out)
    per = seq // tm
    width = tables[0].shape[1]
    assert width == LANES * len(kinds) and tn % width == 0
    specs = [pl.BlockSpec((tm, width), lambda j, i: (i % per, 0))] * 3
    kern = functools.partial(_mm_rope_kernel, kinds=kinds, shift=shift)
    return _mm_call(kern, a, [w], [col_off], list(tables), specs, n_out, BF16, tm, tn, "mm_rope")


def _mm_rms(a, w, col_off, g, out_scale=1.0, tm=1024):
    n = g.shape[0]
    specs = [pl.BlockSpec((1, n), lambda j, i: (0, 0))]
    kern = functools.partial(_mm_rms_kernel, out_scale=out_scale)
    return _mm_call(kern, a, [w], [col_off], [g.reshape(1, n)], specs, n, BF16, tm, n, "mm_rms")


def _mm_swiglu(a, w, d_ff, tm=1024, tn=512):
    return _mm_call(_mm_swiglu_kernel, a, [w, w], [0, d_ff], [], [], d_ff, BF16, tm, tn, "mm_swiglu")


def _mm_resid(a, w, x2, mod3, gate_idx, seq, tm=512, tn=512):
    per = seq // tm
    specs = [pl.BlockSpec((tm, tn), lambda j, i: (i, j)),
             pl.BlockSpec((None, 1, tn), lambda j, i: ((i // per) * 6 + gate_idx, 0, j))]
    return _mm_call(_mm_resid_kernel, a, [w], [0], [x2, mod3], specs, w.shape[1], F32, tm, tn, "mm_resid")


def _rope_tables(seq, rot_dim, pad_to):
    half = rot_dim // 2
    inv_freq = ROPE_THETA ** (-jnp.arange(0, rot_dim, 2, dtype=F32) / rot_dim)
    ang = jnp.arange(seq, dtype=F32)[:, None] * inv_freq[None, :]
    cos, sin = jnp.cos(ang), jnp.sin(ang)
    zero = jnp.zeros_like(sin)
    tail1 = jnp.ones((seq, pad_to - rot_dim), F32)
    tail0 = jnp.zeros((seq, pad_to - rot_dim), F32)
    if 2 * half == pad_to == LANES:
        return (jnp.concatenate([cos, cos], 1), jnp.concatenate([-sin, sin], 1),
                jnp.zeros((seq, pad_to), F32))
    t0 = jnp.concatenate([cos, cos, tail1], 1)
    t1 = jnp.concatenate([-sin, zero, tail0], 1)
    t2 = jnp.concatenate([zero, sin, tail0], 1)
    return t0, t1, t2


DSA_QB = 128
DSA_KB = 512


def _sortable_key(x):
    b = lax.bitcast_convert_type(x, I32)
    return b ^ ((b >> 31) & 0x7FFFFFFF)


def _dsa_kernel(q_ref, iq_ref, wt_ref, k_ref, vt_ref, ik_ref, o_ref,
                keys_ref, bias_ref, acc_ref, m_ref, l_ref, *, top_k):
    qb, kb_sz = DSA_QB, DSA_KB
    start = pl.program_id(1) * qb
    n_blk = (start + qb + kb_sz - 1) // kb_sz

    w_scaled = wt_ref[...] * (IDX_DIM ** -0.5 * IDX_HEADS ** -0.5)
    q_chunk = (start + lax.broadcasted_iota(I32, (kb_sz, qb), 1)) >> CHUNK_SHIFT
    k_iota = lax.broadcasted_iota(I32, (kb_sz, qb), 0)

    def score_blk(kb, carry):
        koff = pl.multiple_of(kb * kb_sz, kb_sz)
        ikb = ik_ref[pl.ds(koff, kb_sz), :]
        acc = jnp.zeros((kb_sz, qb), F32)
        for h in range(IDX_HEADS):
            r = lax.dot_general(ikb, iq_ref[:, h * IDX_DIM:(h + 1) * IDX_DIM], _NT,
                                preferred_element_type=F32)
            acc = acc + jnp.maximum(r, 0.0) * w_scaled[h:h + 1, :]
        visible = ((koff + k_iota) >> CHUNK_SHIFT) <= q_chunk
        keys_ref[pl.ds(koff, kb_sz), :] = _sortable_key(jnp.where(visible, acc, -jnp.inf))
        return carry

    lax.fori_loop(0, n_blk, score_blk, 0)

    def bit_step(b, thr):
        cand = thr + lax.shift_left(jnp.int32(1), 31 - b)

        def count_blk(kb, cnt):
            koff = pl.multiple_of(kb * kb_sz, kb_sz)
            ge = (keys_ref[pl.ds(koff, kb_sz), :] >= cand).astype(I32)
            return cnt + ge.reshape(kb_sz // SUBLANES, SUBLANES, qb).sum(axis=0)

        cnt = lax.fori_loop(0, n_blk, count_blk, jnp.zeros((SUBLANES, qb), I32))
        total = cnt.sum(axis=0, keepdims=True)
        return jnp.where(total >= top_k, cand, thr)

    thr = lax.fori_loop(0, 32, bit_step, jnp.full((1, qb), INT32_MIN, I32))

    def bias_blk(kb, carry):
        koff = pl.multiple_of(kb * kb_sz, kb_sz)
        blk = keys_ref[pl.ds(koff, kb_sz), :]
        bias_ref[pl.ds(koff, kb_sz), :] = jnp.where(
            blk >= thr, jnp.where(blk > NEG_INF_KEY, 0.0, NEG_BIG), NEG_BIG).astype(F32)
        return carry

    lax.fori_loop(0, n_blk, bias_blk, 0)

    m_ref[...] = jnp.full(m_ref.shape, NEG_BIG, F32)
    l_ref[...] = jnp.zeros(l_ref.shape, F32)
    acc_ref[...] = jnp.zeros(acc_ref.shape, F32)

    def att_blk(kb, carry):
        koff = pl.multiple_of(kb * kb_sz, kb_sz)
        bias = bias_ref[pl.ds(koff, kb_sz), :]
        for g in range(DSA_KV_HEADS):
            kblk = k_ref[pl.ds(koff, kb_sz), g * HEAD_DIM:(g + 1) * HEAD_DIM]
            v_t = vt_ref[kb, g * HEAD_DIM:(g + 1) * HEAD_DIM, :]
            for r in range(DSA_GROUP):
                h = g * DSA_GROUP + r
                s_t = lax.dot_general(kblk, q_ref[:, h * HEAD_DIM:(h + 1) * HEAD_DIM], _NT,
                                      preferred_element_type=F32) + bias
                m_prev = m_ref[h]
                m_new = jnp.maximum(m_prev, s_t.max(axis=0, keepdims=True))
                alpha = jnp.exp(m_prev - m_new)
                p = jnp.exp(s_t - m_new)
                l_ref[h] = alpha * l_ref[h] + p.sum(axis=0, keepdims=True)
                acc_ref[h] = alpha * acc_ref[h] + jnp.dot(v_t, p.astype(BF16),
                                                          preferred_element_type=F32)
                m_ref[h] = m_new
        return carry

    lax.fori_loop(0, n_blk, att_blk, 0)
    for h in range(DSA_HEADS):
        o_ref[:, h * HEAD_DIM:(h + 1) * HEAD_DIM] = (acc_ref[h] / l_ref[h]).T.astype(o_ref.dtype)


def _dsa_attention(qk, v_t, iq, ik, iw_t, batch, seq):
    m = qk.shape[0]
    qb, kb_sz = DSA_QB, DSA_KB
    assert seq % kb_sz == 0
    nq = seq // qb
    top_k = min(TOPK_MAX, seq // 4)
    d_q = DSA_HEADS * HEAD_DIM
    d_kv = DSA_KV_HEADS * HEAD_DIM
    return pl.pallas_call(
        functools.partial(_dsa_kernel, top_k=top_k),
        grid=(batch, nq),
        in_specs=[pl.BlockSpec((qb, d_q), lambda b, j: (b * nq + j, 0)),
                  pl.BlockSpec((qb, IDX_HEADS * IDX_DIM), lambda b, j: (b * nq + j, 0)),
                  pl.BlockSpec((IDX_HEADS, qb), lambda b, j: (b, j)),
                  pl.BlockSpec((seq, d_kv), lambda b, j: (b, d_q // d_kv)),
                  pl.BlockSpec((seq // kb_sz, d_kv, kb_sz), lambda b, j: (b, 0, 0)),
                  pl.BlockSpec((seq, IDX_DIM), lambda b, j: (b, 0))],
        out_specs=pl.BlockSpec((qb, d_q), lambda b, j: (b * nq + j, 0)),
        out_shape=jax.ShapeDtypeStruct((m, d_q), BF16),
        scratch_shapes=[pltpu.VMEM((seq, qb), I32),
                        pltpu.VMEM((seq, qb), F32),
                        pltpu.VMEM((DSA_HEADS, HEAD_DIM, qb), F32),
                        pltpu.VMEM((DSA_HEADS, 1, qb), F32),
                        pltpu.VMEM((DSA_HEADS, 1, qb), F32)],
        compiler_params=_params(2),
        name="dsa_attention",
    )(qk, iq, iw_t, qk, v_t, ik)


def _dsa_mixer(h, w_in, batch, seq):
    d_q = DSA_HEADS * HEAD_DIM
    d_kv = DSA_KV_HEADS * HEAD_DIM
    d_iq = IDX_HEADS * IDX_DIM
    off_v = d_q + d_kv
    off_iq = off_v + d_kv
    off_ik = off_iq + d_iq
    off_iw = off_ik + IDX_DIM
    w = w_in
    w_iw = jnp.pad(w[:, off_iw:], ((0, 0), (0, LANES - IDX_HEADS)))
    full_tabs = tuple(t * HEAD_DIM ** -0.25 for t in _rope_tables(seq, HEAD_DIM, HEAD_DIM))
    idx_tabs = _rope_tables(seq, IDX_ROPE_DIM, IDX_DIM)
    qk = _mm_rope(h, w, 0, d_q + d_kv, full_tabs, (True,), HEAD_DIM // 2, seq)
    v = _mm_plain(h, w, off_v, d_kv)
    iq = _mm_rope(h, w, off_iq, d_iq, idx_tabs, (True,), IDX_ROPE_DIM // 2, seq)
    ik = _mm_rope(h, w, off_ik, IDX_DIM, idx_tabs, (True,), IDX_ROPE_DIM // 2, seq)
    iw = _mm_plain(h, w_iw, 0, LANES, out_dtype=F32)
    iw_t = iw[:, :IDX_HEADS].reshape(batch, seq, IDX_HEADS).transpose(0, 2, 1)
    iw_t = iw_t.reshape(batch * IDX_HEADS, seq)
    v_t = v.reshape(batch * seq // DSA_KB, DSA_KB, d_kv).transpose(0, 2, 1)
    return _dsa_attention(qk, v_t, iq, ik, iw_t, batch, seq)


MLA_TQ = 256
MLA_TK = 1024
MLA_HP = 4
MLA_QK_PAD = 2 * LANES
MLA_KV_W = MLA_NOPE_DIM + MLA_V_DIM


def _mla_kernel(q_ref, kv_ref, kr_ref, o_ref, acc_ref, m_ref, l_ref):
    tq, tk = MLA_TQ, MLA_TK
    start = pl.program_id(2) * tq
    n_blk = (start + tq + tk - 1) // tk
    n_full = (start + CHUNK) // tk
    q_chunk = (start + lax.broadcasted_iota(I32, (tq, tk), 0)) >> CHUNK_SHIFT
    k_iota = lax.broadcasted_iota(I32, (tq, tk), 1)
    m_ref[...] = jnp.full(m_ref.shape, NEG_BIG, F32)
    l_ref[...] = jnp.zeros(l_ref.shape, F32)
    acc_ref[...] = jnp.zeros(acc_ref.shape, F32)

    def blk(kb, carry, masked):
        koff = pl.multiple_of(kb * tk, tk)
        kr = kr_ref[pl.ds(koff, tk), :]
        if masked:
            visible = ((koff + k_iota) >> CHUNK_SHIFT) <= q_chunk
        for hh in range(MLA_HP):
            kn = kv_ref[pl.ds(koff, tk), hh * MLA_KV_W:hh * MLA_KV_W + MLA_NOPE_DIM]
            vb = kv_ref[pl.ds(koff, tk), hh * MLA_KV_W + MLA_NOPE_DIM:(hh + 1) * MLA_KV_W]
            q = q_ref[:, hh * MLA_QK_PAD:(hh + 1) * MLA_QK_PAD]
            s = lax.dot_general(q, jnp.concatenate([kn, kr], axis=1), _NT,
                                preferred_element_type=F32)
            if masked:
                s = jnp.where(visible, s, NEG_BIG)
            m_prev = m_ref[hh]
            m_new = jnp.maximum(m_prev, s.max(axis=1, keepdims=True))
            alpha = jnp.exp(m_prev - m_new)
            p = jnp.exp(s - m_new)
            l_ref[hh] = alpha * l_ref[hh] + p.sum(axis=1, keepdims=True)
            acc_ref[hh] = alpha * acc_ref[hh] + jnp.dot(p.astype(BF16), vb,
                                                        preferred_element_type=F32)
            m_ref[hh] = m_new
        return carry

    lax.fori_loop(0, n_full, functools.partial(blk, masked=False), 0)
    lax.fori_loop(n_full, n_blk, functools.partial(blk, masked=True), 0)
    for hh in range(MLA_HP):
        o_ref[:, hh * MLA_V_DIM:(hh + 1) * MLA_V_DIM] = (acc_ref[hh] / l_ref[hh]).astype(o_ref.dtype)


def _mla_attention(q, kv, k_rope, batch, seq):
    m = q.shape[0]
    tq, tk, hp = MLA_TQ, MLA_TK, MLA_HP
    assert seq % tk == 0 and seq % tq == 0 and MLA_HEADS % hp == 0
    nq = seq // tq
    return pl.pallas_call(
        _mla_kernel,
        grid=(batch, MLA_HEADS // hp, nq),
        in_specs=[pl.BlockSpec((tq, hp * MLA_QK_PAD), lambda b, h, i: (b * nq + i, h)),
                  pl.BlockSpec((seq, hp * MLA_KV_W), lambda b, h, i: (b, h)),
                  pl.BlockSpec((seq, LANES), lambda b, h, i: (b, 0))],
        out_specs=pl.BlockSpec((tq, hp * MLA_V_DIM), lambda b, h, i: (b * nq + i, h)),
        out_shape=jax.ShapeDtypeStruct((m, MLA_HEADS * MLA_V_DIM), BF16),
        scratch_shapes=[pltpu.VMEM((hp, tq, MLA_V_DIM), F32),
                        pltpu.VMEM((hp, tq, 1), F32),
                        pltpu.VMEM((hp, tq, 1), F32)],
        compiler_params=_params(3),
        name="mla_attention",
    )(q, kv, k_rope)


def _mla_mixer(h, w_down, q_norm, kv_norm, w_uq, w_ukv, batch, seq):
    wd = w_down
    wd_rope = jnp.pad(wd[:, MLA_Q_RANK + MLA_KV_RANK:], ((0, 0), (0, LANES - MLA_ROPE_DIM)))
    w_uq_h = w_uq.reshape(MLA_Q_RANK, MLA_HEADS, MLA_NOPE_DIM + MLA_ROPE_DIM)
    pad = MLA_QK_PAD - MLA_NOPE_DIM - MLA_ROPE_DIM
    w_uq_p = jnp.pad(w_uq_h, ((0, 0), (0, 0), (0, pad))).reshape(MLA_Q_RANK, MLA_HEADS * MLA_QK_PAD)
    rope_tabs = _rope_tables(seq, MLA_ROPE_DIM, LANES)
    q_tabs = tuple(jnp.concatenate([jnp.zeros_like(t), t], axis=1) for t in rope_tabs)

    cq = _mm_rms(h, wd, 0, q_norm, out_scale=(MLA_NOPE_DIM + MLA_ROPE_DIM) ** -0.5)
    ckv = _mm_rms(h, wd, MLA_Q_RANK, kv_norm)
    k_rope = _mm_rope(h, wd_rope, 0, LANES, rope_tabs, (True,), MLA_ROPE_DIM // 2, seq)
    q = _mm_rope(cq, w_uq_p, 0, MLA_HEADS * MLA_QK_PAD, q_tabs, (False, True), MLA_ROPE_DIM // 2, seq)
    kv = _mm_plain(ckv, w_ukv, 0, MLA_HEADS * (MLA_NOPE_DIM + MLA_V_DIM))
    return _mla_attention(q, kv, k_rope, batch, seq)


SB_T = 256
SB_SUB = 256
SB_HP = 4
SB_EXP_ZERO_BELOW = -104.0


def _sb_kernel(q_ref, k_ref, v_ref, o_ref, acc_ref, c_ref):
    t, sub = SB_T, SB_SUB
    n_sub = t // sub
    i = pl.program_id(2)
    scale = HEAD_DIM ** -0.5
    row = lax.broadcasted_iota(I32, (t, t), 0)
    col = lax.broadcasted_iota(I32, (t, t), 1)
    later = (row > col).astype(BF16)
    q_i = lax.broadcasted_iota(I32, (sub, t), 0)
    k_i = lax.broadcasted_iota(I32, (sub, t), 1)
    acc_ref[...] = jnp.zeros(acc_ref.shape, F32)
    c_ref[...] = jnp.zeros(c_ref.shape, F32)

    def step(n, diagonal):
        koff = pl.multiple_of((i - n) * t, t)
        c_max = jnp.float32(-jnp.inf)
        for hh in range(SB_HP):
            lanes = slice(hh * HEAD_DIM, (hh + 1) * HEAD_DIM)
            kblk = k_ref[pl.ds(koff, t), lanes]
            vblk = v_ref[pl.ds(koff, t), lanes]
            for cc in range(n_sub):
                idx = hh * n_sub + cc
                z = lax.dot_general(q_ref[cc * sub:(cc + 1) * sub, lanes], kblk, _NT,
                                    preferred_element_type=F32) * scale
                sp = jnp.log(1.0 + jnp.exp(-jnp.abs(z)))
                log_beta = jnp.minimum(z, 0.0) - sp
                log_keep = jnp.minimum(-z, 0.0) - sp
                if diagonal:
                    before = k_i < cc * sub + q_i
                    log_keep = jnp.where(before, log_keep, 0.0)
                hi = log_keep.astype(BF16)
                lo = (log_keep - hi.astype(F32)).astype(BF16)
                within = (jnp.dot(hi, later, preferred_element_type=F32)
                          + jnp.dot(lo, later, preferred_element_type=F32))
                c = c_ref[idx]
                a = jnp.exp(log_beta + within + c)
                if diagonal:
                    a = jnp.where(before, a, 0.0)
                acc_ref[idx] += jnp.dot(a.astype(BF16), vblk, preferred_element_type=F32)
                c = c + log_keep.sum(axis=1, keepdims=True)
                c_ref[idx] = c
                c_max = jnp.maximum(c_max, jnp.max(c))
        return c_max

    def cond(state):
        n, c_max = state
        return jnp.logical_and(n <= i, c_max >= SB_EXP_ZERO_BELOW)

    def body(state):
        n, _ = state
        return n + 1, step(n, diagonal=False)

    lax.while_loop(cond, body, (jnp.int32(1), step(0, diagonal=True)))
    for hh in range(SB_HP):
        for cc in range(n_sub):
            o_ref[cc * sub:(cc + 1) * sub, hh * HEAD_DIM:(hh + 1) * HEAD_DIM] = (
                acc_ref[hh * n_sub + cc].astype(o_ref.dtype))


def _sb_attention(qkv, batch, seq):
    m = qkv.shape[0]
    t, hp = SB_T, SB_HP
    assert seq % t == 0 and SB_HEADS % hp == 0
    nq = seq // t
    ng = SB_HEADS // hp
    w = hp * HEAD_DIM
    return pl.pallas_call(
        _sb_kernel,
        grid=(batch, ng, nq),
        in_specs=[pl.BlockSpec((t, w), lambda b, h, i: (b * nq + i, h)),
                  pl.BlockSpec((seq, w), lambda b, h, i: (b, ng + h)),
                  pl.BlockSpec((seq, w), lambda b, h, i: (b, 2 * ng + h))],
        out_specs=pl.BlockSpec((t, w), lambda b, h, i: (b * nq + i, h)),
        out_shape=jax.ShapeDtypeStruct((m, SB_HEADS * HEAD_DIM), BF16),
        scratch_shapes=[pltpu.VMEM((hp * t // SB_SUB, SB_SUB, HEAD_DIM), F32),
                        pltpu.VMEM((hp * t // SB_SUB, SB_SUB, 1), F32)],
        compiler_params=_params(3),
        name="sb_attention",
    )(qkv, qkv, qkv)


def _sb_mixer(h, w_in, batch, seq):
    qkv = _mm_plain(h, w_in, 0, 3 * SB_HEADS * HEAD_DIM)
    return _sb_attention(qkv, batch, seq)


def _trunk(x, c, layers, final_norm):
    batch, seq, d = x.shape
    c_pad = jnp.pad(c, ((0, SUBLANES - batch), (0, 0)))
    x2 = x.reshape(batch * seq, d)
    for kind, ada_w, ada_b, g_mix, g_ffn, mix, w_ff_in, w_ff_out in layers:
        mod3 = _adaln(c_pad, ada_w, ada_b).reshape(SUBLANES * 6, 1, d)
        h = _norm_mod(x2, g_mix, mod3, 1, 0, seq)
        if kind == 0:
            o = _dsa_mixer(h, mix[0], batch, seq)
        elif kind == 1:
            o = _mla_mixer(h, *mix[:5], batch, seq)
        else:
            o = _sb_mixer(h, mix[0], batch, seq)
        x2 = _mm_resid(o, mix[-1], x2, mod3, 2, seq)
        h = _norm_mod(x2, g_ffn, mod3, 4, 3, seq)
        d_ff = w_ff_out.shape[0]
        u = _mm_swiglu(h, w_ff_in, d_ff)
        x2 = _mm_resid(u, w_ff_out, x2, mod3, 5, seq)
    return _norm_final(x2, final_norm).reshape(batch, seq, d)


def kernel(x, c, ada_w_0, ada_b_0, norm_mix_0, norm_ffn_0, dsa_w_in_0, dsa_w_out_0, ffn_w_in_0, ffn_w_out_0, ada_w_1, ada_b_1, norm_mix_1, norm_ffn_1, mla_w_down_1, mla_q_norm_1, mla_kv_norm_1, mla_w_uq_1, mla_w_ukv_1, mla_w_out_1, ffn_w_in_1, ffn_w_out_1, ada_w_2, ada_b_2, norm_mix_2, norm_ffn_2, sb_w_in_2, sb_w_out_2, ffn_w_in_2, ffn_w_out_2, ada_w_3, ada_b_3, norm_mix_3, norm_ffn_3, dsa_w_in_3, dsa_w_out_3, ffn_w_in_3, ffn_w_out_3, final_norm):
    layers = (
        (0, ada_w_0, ada_b_0, norm_mix_0, norm_ffn_0, (dsa_w_in_0, dsa_w_out_0), ffn_w_in_0, ffn_w_out_0),
        (1, ada_w_1, ada_b_1, norm_mix_1, norm_ffn_1,
         (mla_w_down_1, mla_q_norm_1, mla_kv_norm_1, mla_w_uq_1, mla_w_ukv_1, mla_w_out_1),
         ffn_w_in_1, ffn_w_out_1),
        (2, ada_w_2, ada_b_2, norm_mix_2, norm_ffn_2, (sb_w_in_2, sb_w_out_2), ffn_w_in_2, ffn_w_out_2),
        (0, ada_w_3, ada_b_3, norm_mix_3, norm_ffn_3, (dsa_w_in_3, dsa_w_out_3), ffn_w_in_3, ffn_w_out_3),
    )
    return _trunk(x, c, layers, final_norm)
```

```python
import functools
import math

import numpy as np
import jax
import jax.numpy as jnp
from jax import lax
from jax.experimental import pallas as pl
from jax.experimental.pallas import tpu as pltpu

F32 = jnp.float32
BF16 = jnp.bfloat16
I32 = jnp.int32

V7X_VMEM_BYTES = 64 * 1024 * 1024
VMEM_LIMIT_BYTES = V7X_VMEM_BYTES * 7 // 8
LANES = 128
SUBLANES = 8

CHUNK = 64
CHUNK_SHIFT = int(math.log2(CHUNK))
assert 1 << CHUNK_SHIFT == CHUNK
ROPE_THETA = 10000.0
NORM_EPS = 1e-6
DSA_HEADS = 16
DSA_KV_HEADS = 4
DSA_GROUP = DSA_HEADS // DSA_KV_HEADS
HEAD_DIM = 128
IDX_HEADS = 16
IDX_DIM = 128
IDX_ROPE_DIM = 64
TOPK_MAX = 256
MLA_HEADS = 16
MLA_Q_RANK = 512
MLA_KV_RANK = 512
MLA_NOPE_DIM = 128
MLA_ROPE_DIM = 64
MLA_V_DIM = 128
SB_HEADS = 16

NEG_BIG = -1e30
INT32_MIN = -(2 ** 31)
NEG_INF_KEY = int(np.int32(np.uint32(0xFF800000) ^ np.uint32(0x7FFFFFFF)))

_NT = (((1,), (1,)), ((), ()))


def _params(n_grid_dims):
    return pltpu.CompilerParams(
        dimension_semantics=("arbitrary",) * n_grid_dims,
        vmem_limit_bytes=VMEM_LIMIT_BYTES)


def _sigmoid(x):
    return 1.0 / (1.0 + jnp.exp(-x))


def _adaln_kernel(c_ref, w_ref, b_ref, o_ref):
    c = c_ref[...]
    s = (c * _sigmoid(c)).astype(BF16)
    o_ref[...] = jnp.dot(s, w_ref[...].astype(BF16), preferred_element_type=F32) + b_ref[...]


def _adaln(c_pad, w, b):
    rows, d = c_pad.shape
    n = w.shape[1]
    tn = 1024
    return pl.pallas_call(
        _adaln_kernel,
        grid=(n // tn,),
        in_specs=[pl.BlockSpec((rows, d), lambda j: (0, 0)),
                  pl.BlockSpec((d, tn), lambda j: (0, j)),
                  pl.BlockSpec((1, tn), lambda j: (0, j))],
        out_specs=pl.BlockSpec((rows, tn), lambda j: (0, j)),
        out_shape=jax.ShapeDtypeStruct((rows, n), F32),
        compiler_params=_params(1),
        name="adaln",
    )(c_pad, w, b.reshape(1, n))


def _rms(x, g):
    return x * lax.rsqrt(jnp.mean(x * x, axis=-1, keepdims=True) + NORM_EPS) * g


def _norm_mod_kernel(x_ref, g_ref, sc_ref, sh_ref, o_ref):
    y = _rms(x_ref[...], g_ref[...])
    o_ref[...] = (y * (1.0 + sc_ref[...]) + sh_ref[...]).astype(o_ref.dtype)


def _norm_kernel(x_ref, g_ref, o_ref):
    o_ref[...] = _rms(x_ref[...], g_ref[...]).astype(o_ref.dtype)


def _norm_mod(x2, g, mod3, sc_idx, sh_idx, seq):
    m, d = x2.shape
    ts = 512
    per = seq // ts
    return pl.pallas_call(
        _norm_mod_kernel,
        grid=(m // ts,),
        in_specs=[pl.BlockSpec((ts, d), lambda i: (i, 0)),
                  pl.BlockSpec((1, d), lambda i: (0, 0)),
                  pl.BlockSpec((None, 1, d), lambda i: ((i // per) * 6 + sc_idx, 0, 0)),
                  pl.BlockSpec((None, 1, d), lambda i: ((i // per) * 6 + sh_idx, 0, 0))],
        out_specs=pl.BlockSpec((ts, d), lambda i: (i, 0)),
        out_shape=jax.ShapeDtypeStruct((m, d), BF16),
        compiler_params=_params(1),
        name="norm_mod",
    )(x2, g.reshape(1, d), mod3, mod3)


def _norm_final(x2, g):
    m, d = x2.shape
    ts = 512
    return pl.pallas_call(
        _norm_kernel,
        grid=(m // ts,),
        in_specs=[pl.BlockSpec((ts, d), lambda i: (i, 0)),
                  pl.BlockSpec((1, d), lambda i: (0, 0))],
        out_specs=pl.BlockSpec((ts, d), lambda i: (i, 0)),
        out_shape=jax.ShapeDtypeStruct((m, d), F32),
        compiler_params=_params(1),
        name="norm_final",
    )(x2, g.reshape(1, d))


def _dot(a_ref, w_ref, wb_ref):
    @pl.when(pl.program_id(1) == 0)
    def _():
        wb_ref[...] = w_ref[...].astype(BF16)

    return jnp.dot(a_ref[...], wb_ref[...], preferred_element_type=F32)


def _mm_plain_kernel(a_ref, w_ref, o_ref, wb_ref):
    o_ref[...] = _dot(a_ref, w_ref, wb_ref).astype(o_ref.dtype)


def _mm_rope_kernel(a_ref, w_ref, t0_ref, t1_ref, t2_ref, o_ref, wb_ref, *, kinds, shift):
    acc = _dot(a_ref, w_ref, wb_ref)
    period = len(kinds)
    for c in range(acc.shape[1] // LANES):
        y = acc[:, c * LANES:(c + 1) * LANES]
        k = c % period
        if kinds[k]:
            t = slice(k * LANES, (k + 1) * LANES)
            out = y * t0_ref[:, t] + pltpu.roll(y, LANES - shift, 1) * t1_ref[:, t]
            if 2 * shift != LANES:
                out = out + pltpu.roll(y, shift, 1) * t2_ref[:, t]
        else:
            out = y
        o_ref[:, c * LANES:(c + 1) * LANES] = out.astype(o_ref.dtype)


def _mm_rms_kernel(a_ref, w_ref, g_ref, o_ref, wb_ref, *, out_scale):
    o_ref[...] = (_rms(_dot(a_ref, w_ref, wb_ref), g_ref[...]) * out_scale).astype(o_ref.dtype)


def _mm_swiglu_kernel(a_ref, wg_ref, wu_ref, o_ref, wgb_ref, wub_ref):
    g = _dot(a_ref, wg_ref, wgb_ref)
    u = _dot(a_ref, wu_ref, wub_ref)
    o_ref[...] = (g * _sigmoid(g) * u).astype(o_ref.dtype)


def _mm_resid_kernel(a_ref, w_ref, x_ref, gt_ref, o_ref, wb_ref):
    o_ref[...] = x_ref[...] + gt_ref[...] * _dot(a_ref, w_ref, wb_ref)


def _mm_call(kernel, a, weights, w_offs, extras, extra_specs, n_out, out_dtype, tm, tn, name):
    m, k = a.shape
    assert m % tm == 0 and n_out % tn == 0
    in_specs = [pl.BlockSpec((tm, k), lambda j, i: (i, 0))]
    for off in w_offs:
        assert off % tn == 0
        in_specs.append(pl.BlockSpec((k, tn), lambda j, i, o=off // tn: (0, j + o)))
    in_specs += extra_specs
    return pl.pallas_call(
        kernel,
        grid=(n_out // tn, m // tm),
        in_specs=in_specs,
        out_specs=pl.BlockSpec((tm, tn), lambda j, i: (i, j)),
        out_shape=jax.ShapeDtypeStruct((m, n_out), out_dtype),
        scratch_shapes=[pltpu.VMEM((k, tn), BF16)] * len(weights),
        compiler_params=_params(2),
        name=name,
    )(a, *weights, *extras)


def _mm_plain(a, w, col_off, n_out, out_dtype=BF16, tm=1024, tn=512):
    tn = min(tn, n_out)
    return _mm_call(_mm_plain_kernel, a, [w], [col_off], [], [], n_out, out_dtype, tm, tn, "mm_plain")


def _mm_rope(a, w, col_off, n_out, tables, kinds, shift, seq, tm=1024, tn=512):
    tn = min(tn, n_out)
    per = seq // tm
    width = tables[0].shape[1]
    assert width == LANES * len(kinds) and tn % width == 0
    specs = [pl.BlockSpec((tm, width), lambda j, i: (i % per, 0))] * 3
    kern = functools.partial(_mm_rope_kernel, kinds=kinds, shift=shift)
    return _mm_call(kern, a, [w], [col_off], list(tables), specs, n_out, BF16, tm, tn, "mm_rope")


def _mm_rms(a, w, col_off, g, out_scale=1.0, tm=1024):
    n = g.shape[0]
    specs = [pl.BlockSpec((1, n), lambda j, i: (0, 0))]
    kern = functools.partial(_mm_rms_kernel, out_scale=out_scale)
    return _mm_call(kern, a, [w], [col_off], [g.reshape(1, n)], specs, n, BF16, tm, n, "mm_rms")


def _mm_swiglu(a, w, d_ff, tm=1024, tn=512):
    return _mm_call(_mm_swiglu_kernel, a, [w, w], [0, d_ff], [], [], d_ff, BF16, tm, tn, "mm_swiglu")


MIXER_OUT_TN = 1024


def _mm_resid(a, w, x2, mod3, gate_idx, seq, tm=512, tn=512):
    per = seq // tm
    specs = [pl.BlockSpec((tm, tn), lambda j, i: (i, j)),
             pl.BlockSpec((None, 1, tn), lambda j, i: ((i // per) * 6 + gate_idx, 0, j))]
    return _mm_call(_mm_resid_kernel, a, [w], [0], [x2, mod3], specs, w.shape[1], F32, tm, tn, "mm_resid")


def _rope_tables(seq, rot_dim, pad_to):
    half = rot_dim // 2
    inv_freq = ROPE_THETA ** (-jnp.arange(0, rot_dim, 2, dtype=F32) / rot_dim)
    ang = jnp.arange(seq, dtype=F32)[:, None] * inv_freq[None, :]
    cos, sin = jnp.cos(ang), jnp.sin(ang)
    zero = jnp.zeros_like(sin)
    tail1 = jnp.ones((seq, pad_to - rot_dim), F32)
    tail0 = jnp.zeros((seq, pad_to - rot_dim), F32)
    if 2 * half == pad_to == LANES:
        return (jnp.concatenate([cos, cos], 1), jnp.concatenate([-sin, sin], 1),
                jnp.zeros((seq, pad_to), F32))
    t0 = jnp.concatenate([cos, cos, tail1], 1)
    t1 = jnp.concatenate([-sin, zero, tail0], 1)
    t2 = jnp.concatenate([zero, sin, tail0], 1)
    return t0, t1, t2


DSA_QB = 128
DSA_KB = 512


def _sortable_key(x):
    b = lax.bitcast_convert_type(x, I32)
    return b ^ ((b >> 31) & 0x7FFFFFFF)


def _dsa_kernel(q_ref, iq_ref, wt_ref, k_ref, vt_ref, ik_ref, o_ref,
                keys_ref, bias_ref, acc_ref, m_ref, l_ref, *, top_k):
    qb, kb_sz = DSA_QB, DSA_KB
    start = pl.program_id(1) * qb
    n_blk = (start + qb + kb_sz - 1) // kb_sz

    w_scaled = wt_ref[...] * (IDX_DIM ** -0.5 * IDX_HEADS ** -0.5)
    q_chunk = (start + lax.broadcasted_iota(I32, (kb_sz, qb), 1)) >> CHUNK_SHIFT
    k_iota = lax.broadcasted_iota(I32, (kb_sz, qb), 0)

    def score_blk(kb, carry):
        koff = pl.multiple_of(kb * kb_sz, kb_sz)
        ikb = ik_ref[pl.ds(koff, kb_sz), :]
        acc = jnp.zeros((kb_sz, qb), F32)
        for h in range(IDX_HEADS):
            r = lax.dot_general(ikb, iq_ref[:, h * IDX_DIM:(h + 1) * IDX_DIM], _NT,
                                preferred_element_type=F32)
            acc = acc + jnp.maximum(r, 0.0) * w_scaled[h:h + 1, :]
        visible = ((koff + k_iota) >> CHUNK_SHIFT) <= q_chunk
        keys_ref[pl.ds(koff, kb_sz), :] = _sortable_key(jnp.where(visible, acc, -jnp.inf))
        return carry

    lax.fori_loop(0, n_blk, score_blk, 0)

    def bit_step(b, thr):
        cand = thr + lax.shift_left(jnp.int32(1), 31 - b)

        def count_blk(kb, cnt):
            koff = pl.multiple_of(kb * kb_sz, kb_sz)
            ge = (keys_ref[pl.ds(koff, kb_sz), :] >= cand).astype(I32)
            return cnt + ge.reshape(kb_sz // SUBLANES, SUBLANES, qb).sum(axis=0)

        cnt = lax.fori_loop(0, n_blk, count_blk, jnp.zeros((SUBLANES, qb), I32))
        total = cnt.sum(axis=0, keepdims=True)
        return jnp.where(total >= top_k, cand, thr)

    thr = lax.fori_loop(0, 32, bit_step, jnp.full((1, qb), INT32_MIN, I32))

    def bias_blk(kb, carry):
        koff = pl.multiple_of(kb * kb_sz, kb_sz)
        blk = keys_ref[pl.ds(koff, kb_sz), :]
        bias_ref[pl.ds(koff, kb_sz), :] = jnp.where(
            blk >= thr, jnp.where(blk > NEG_INF_KEY, 0.0, NEG_BIG), NEG_BIG).astype(F32)
        return carry

    lax.fori_loop(0, n_blk, bias_blk, 0)

    m_ref[...] = jnp.full(m_ref.shape, NEG_BIG, F32)
    l_ref[...] = jnp.zeros(l_ref.shape, F32)
    acc_ref[...] = jnp.zeros(acc_ref.shape, F32)

    def att_blk(kb, carry):
        koff = pl.multiple_of(kb * kb_sz, kb_sz)
        bias = bias_ref[pl.ds(koff, kb_sz), :]
        for g in range(DSA_KV_HEADS):
            kblk = k_ref[pl.ds(koff, kb_sz), g * HEAD_DIM:(g + 1) * HEAD_DIM]
            v_t = vt_ref[kb, g * HEAD_DIM:(g + 1) * HEAD_DIM, :]
            for r in range(DSA_GROUP):
                h = g * DSA_GROUP + r
                s_t = lax.dot_general(kblk, q_ref[:, h * HEAD_DIM:(h + 1) * HEAD_DIM], _NT,
                                      preferred_element_type=F32) + bias
                m_prev = m_ref[h]
                m_new = jnp.maximum(m_prev, s_t.max(axis=0, keepdims=True))
                alpha = jnp.exp(m_prev - m_new)
                p = jnp.exp(s_t - m_new)
                l_ref[h] = alpha * l_ref[h] + p.sum(axis=0, keepdims=True)
                acc_ref[h] = alpha * acc_ref[h] + jnp.dot(v_t, p.astype(BF16),
                                                          preferred_element_type=F32)
                m_ref[h] = m_new
        return carry

    lax.fori_loop(0, n_blk, att_blk, 0)
    for h in range(DSA_HEADS):
        o_ref[:, h * HEAD_DIM:(h + 1) * HEAD_DIM] = (acc_ref[h] / l_ref[h]).T.astype(o_ref.dtype)


def _dsa_attention(qk, v_t, iq, ik, iw_t, batch, seq):
    m = qk.shape[0]
    qb, kb_sz = DSA_QB, DSA_KB
    assert seq % kb_sz == 0
    nq = seq // qb
    top_k = min(TOPK_MAX, seq // 4)
    d_q = DSA_HEADS * HEAD_DIM
    d_kv = DSA_KV_HEADS * HEAD_DIM
    return pl.pallas_call(
        functools.partial(_dsa_kernel, top_k=top_k),
        grid=(batch, nq),
        in_specs=[pl.BlockSpec((qb, d_q), lambda b, j: (b * nq + j, 0)),
                  pl.BlockSpec((qb, IDX_HEADS * IDX_DIM), lambda b, j: (b * nq + j, 0)),
                  pl.BlockSpec((IDX_HEADS, qb), lambda b, j: (b, j)),
                  pl.BlockSpec((seq, d_kv), lambda b, j: (b, d_q // d_kv)),
                  pl.BlockSpec((seq // kb_sz, d_kv, kb_sz), lambda b, j: (b, 0, 0)),
                  pl.BlockSpec((seq, IDX_DIM), lambda b, j: (b, 0))],
        out_specs=pl.BlockSpec((qb, d_q), lambda b, j: (b * nq + j, 0)),
        out_shape=jax.ShapeDtypeStruct((m, d_q), BF16),
        scratch_shapes=[pltpu.VMEM((seq, qb), I32),
                        pltpu.VMEM((seq, qb), F32),
                        pltpu.VMEM((DSA_HEADS, HEAD_DIM, qb), F32),
                        pltpu.VMEM((DSA_HEADS, 1, qb), F32),
                        pltpu.VMEM((DSA_HEADS, 1, qb), F32)],
        compiler_params=_params(2),
        name="dsa_attention",
    )(qk, iq, iw_t, qk, v_t, ik)


def _dsa_mixer(h, w_in, batch, seq):
    d_q = DSA_HEADS * HEAD_DIM
    d_kv = DSA_KV_HEADS * HEAD_DIM
    d_iq = IDX_HEADS * IDX_DIM
    off_v = d_q + d_kv
    off_iq = off_v + d_kv
    off_ik = off_iq + d_iq
    off_iw = off_ik + IDX_DIM
    w = w_in
    w_iw = jnp.pad(w[:, off_iw:], ((0, 0), (0, LANES - IDX_HEADS)))
    full_tabs = tuple(t * HEAD_DIM ** -0.25 for t in _rope_tables(seq, HEAD_DIM, HEAD_DIM))
    idx_tabs = _rope_tables(seq, IDX_ROPE_DIM, IDX_DIM)
    qk = _mm_rope(h, w, 0, d_q + d_kv, full_tabs, (True,), HEAD_DIM // 2, seq)
    v = _mm_plain(h, w, off_v, d_kv)
    iq = _mm_rope(h, w, off_iq, d_iq, idx_tabs, (True,), IDX_ROPE_DIM // 2, seq)
    ik = _mm_rope(h, w, off_ik, IDX_DIM, idx_tabs, (True,), IDX_ROPE_DIM // 2, seq)
    iw = _mm_plain(h, w_iw, 0, LANES, out_dtype=F32)
    iw_t = iw[:, :IDX_HEADS].reshape(batch, seq, IDX_HEADS).transpose(0, 2, 1)
    iw_t = iw_t.reshape(batch * IDX_HEADS, seq)
    v_t = v.reshape(batch * seq // DSA_KB, DSA_KB, d_kv).transpose(0, 2, 1)
    return _dsa_attention(qk, v_t, iq, ik, iw_t, batch, seq)


MLA_TQ = 256
MLA_TK = 1024
MLA_HP = 4
MLA_QK_PAD = 2 * LANES
MLA_KV_W = MLA_NOPE_DIM + MLA_V_DIM


def _mla_kernel(q_ref, kv_ref, kr_ref, o_ref, acc_ref, m_ref, l_ref):
    tq, tk = MLA_TQ, MLA_TK
    start = pl.program_id(2) * tq
    n_blk = (start + tq + tk - 1) // tk
    n_full = (start + CHUNK) // tk
    q_chunk = (start + lax.broadcasted_iota(I32, (tq, tk), 0)) >> CHUNK_SHIFT
    k_iota = lax.broadcasted_iota(I32, (tq, tk), 1)
    m_ref[...] = jnp.full(m_ref.shape, NEG_BIG, F32)
    l_ref[...] = jnp.zeros(l_ref.shape, F32)
    acc_ref[...] = jnp.zeros(acc_ref.shape, F32)

    def blk(kb, carry, masked):
        koff = pl.multiple_of(kb * tk, tk)
        kr = kr_ref[pl.ds(koff, tk), :]
        if masked:
            visible = ((koff + k_iota) >> CHUNK_SHIFT) <= q_chunk
        for hh in range(MLA_HP):
            kn = kv_ref[pl.ds(koff, tk), hh * MLA_KV_W:hh * MLA_KV_W + MLA_NOPE_DIM]
            vb = kv_ref[pl.ds(koff, tk), hh * MLA_KV_W + MLA_NOPE_DIM:(hh + 1) * MLA_KV_W]
            q = q_ref[:, hh * MLA_QK_PAD:(hh + 1) * MLA_QK_PAD]
            s = lax.dot_general(q, jnp.concatenate([kn, kr], axis=1), _NT,
                                preferred_element_type=F32)
            if masked:
                s = jnp.where(visible, s, NEG_BIG)
            m_prev = m_ref[hh]
            m_new = jnp.maximum(m_prev, s.max(axis=1, keepdims=True))
            alpha = jnp.exp(m_prev - m_new)
            p = jnp.exp(s - m_new)
            l_ref[hh] = alpha * l_ref[hh] + p.sum(axis=1, keepdims=True)
            acc_ref[hh] = alpha * acc_ref[hh] + jnp.dot(p.astype(BF16), vb,
                                                        preferred_element_type=F32)
            m_ref[hh] = m_new
        return carry

    lax.fori_loop(0, n_full, functools.partial(blk, masked=False), 0)
    lax.fori_loop(n_full, n_blk, functools.partial(blk, masked=True), 0)
    for hh in range(MLA_HP):
        o_ref[:, hh * MLA_V_DIM:(hh + 1) * MLA_V_DIM] = (acc_ref[hh] / l_ref[hh]).astype(o_ref.dtype)


def _mla_attention(q, kv, k_rope, batch, seq):
    m = q.shape[0]
    tq, tk, hp = MLA_TQ, MLA_TK, MLA_HP
    assert seq % tk == 0 and seq % tq == 0 and MLA_HEADS % hp == 0
    nq = seq // tq
    return pl.pallas_call(
        _mla_kernel,
        grid=(batch, MLA_HEADS // hp, nq),
        in_specs=[pl.BlockSpec((tq, hp * MLA_QK_PAD), lambda b, h, i: (b * nq + i, h)),
                  pl.BlockSpec((seq, hp * MLA_KV_W), lambda b, h, i: (b, h)),
                  pl.BlockSpec((seq, LANES), lambda b, h, i: (b, 0))],
        out_specs=pl.BlockSpec((tq, hp * MLA_V_DIM), lambda b, h, i: (b * nq + i, h)),
        out_shape=jax.ShapeDtypeStruct((m, MLA_HEADS * MLA_V_DIM), BF16),
        scratch_shapes=[pltpu.VMEM((hp, tq, MLA_V_DIM), F32),
                        pltpu.VMEM((hp, tq, 1), F32),
                        pltpu.VMEM((hp, tq, 1), F32)],
        compiler_params=_params(3),
        name="mla_attention",
    )(q, kv, k_rope)


def _mla_mixer(h, w_down, q_norm, kv_norm, w_uq, w_ukv, batch, seq):
    wd = w_down
    wd_rope = jnp.pad(wd[:, MLA_Q_RANK + MLA_KV_RANK:], ((0, 0), (0, LANES - MLA_ROPE_DIM)))
    w_uq_h = w_uq.reshape(MLA_Q_RANK, MLA_HEADS, MLA_NOPE_DIM + MLA_ROPE_DIM)
    pad = MLA_QK_PAD - MLA_NOPE_DIM - MLA_ROPE_DIM
    w_uq_p = jnp.pad(w_uq_h, ((0, 0), (0, 0), (0, pad))).reshape(MLA_Q_RANK, MLA_HEADS * MLA_QK_PAD)
    rope_tabs = _rope_tables(seq, MLA_ROPE_DIM, LANES)
    q_tabs = tuple(jnp.concatenate([jnp.zeros_like(t), t], axis=1) for t in rope_tabs)

    cq = _mm_rms(h, wd, 0, q_norm, out_scale=(MLA_NOPE_DIM + MLA_ROPE_DIM) ** -0.5)
    ckv = _mm_rms(h, wd, MLA_Q_RANK, kv_norm)
    k_rope = _mm_rope(h, wd_rope, 0, LANES, rope_tabs, (True,), MLA_ROPE_DIM // 2, seq)
    q = _mm_rope(cq, w_uq_p, 0, MLA_HEADS * MLA_QK_PAD, q_tabs, (False, True), MLA_ROPE_DIM // 2, seq)
    kv = _mm_plain(ckv, w_ukv, 0, MLA_HEADS * (MLA_NOPE_DIM + MLA_V_DIM))
    return _mla_attention(q, kv, k_rope, batch, seq)


SB_T = 256
SB_SUB = 256
SB_HP = 4
SB_EXP_ZERO_BELOW = -104.0


def _sb_kernel(q_ref, k_ref, v_ref, o_ref, acc_ref, c_ref):
    t, sub = SB_T, SB_SUB
    n_sub = t // sub
    i = pl.program_id(2)
    scale = HEAD_DIM ** -0.5
    row = lax.broadcasted_iota(I32, (t, t), 0)
    col = lax.broadcasted_iota(I32, (t, t), 1)
    later = (row > col).astype(BF16)
    q_i = lax.broadcasted_iota(I32, (sub, t), 0)
    k_i = lax.broadcasted_iota(I32, (sub, t), 1)
    acc_ref[...] = jnp.zeros(acc_ref.shape, F32)
    c_ref[...] = jnp.zeros(c_ref.shape, F32)

    def step(n, diagonal):
        koff = pl.multiple_of((i - n) * t, t)
        c_max = jnp.float32(-jnp.inf)
        for hh in range(SB_HP):
            lanes = slice(hh * HEAD_DIM, (hh + 1) * HEAD_DIM)
            kblk = k_ref[pl.ds(koff, t), lanes]
            vblk = v_ref[pl.ds(koff, t), lanes]
            for cc in range(n_sub):
                idx = hh * n_sub + cc
                z = lax.dot_general(q_ref[cc * sub:(cc + 1) * sub, lanes], kblk, _NT,
                                    preferred_element_type=F32) * scale
                sp = jnp.log(1.0 + jnp.exp(-jnp.abs(z)))
                log_beta = jnp.minimum(z, 0.0) - sp
                log_keep = jnp.minimum(-z, 0.0) - sp
                if diagonal:
                    before = k_i < cc * sub + q_i
                    log_keep = jnp.where(before, log_keep, 0.0)
                hi = log_keep.astype(BF16)
                lo = (log_keep - hi.astype(F32)).astype(BF16)
                within = (jnp.dot(hi, later, preferred_element_type=F32)
                          + jnp.dot(lo, later, preferred_element_type=F32))
                c = c_ref[idx]
                a = jnp.exp(log_beta + within + c)
                if diagonal:
                    a = jnp.where(before, a, 0.0)
                acc_ref[idx] += jnp.dot(a.astype(BF16), vblk, preferred_element_type=F32)
                c = c + log_keep.sum(axis=1, keepdims=True)
                c_ref[idx] = c
                c_max = jnp.maximum(c_max, jnp.max(c))
        return c_max

    def cond(state):
        n, c_max = state
        return jnp.logical_and(n <= i, c_max >= SB_EXP_ZERO_BELOW)

    def body(state):
        n, _ = state
        return n + 1, step(n, diagonal=False)

    lax.while_loop(cond, body, (jnp.int32(1), step(0, diagonal=True)))
    for hh in range(SB_HP):
        for cc in range(n_sub):
            o_ref[cc * sub:(cc + 1) * sub, hh * HEAD_DIM:(hh + 1) * HEAD_DIM] = (
                acc_ref[hh * n_sub + cc].astype(o_ref.dtype))


def _sb_attention(qkv, batch, seq):
    m = qkv.shape[0]
    t, hp = SB_T, SB_HP
    assert seq % t == 0 and SB_HEADS % hp == 0
    nq = seq // t
    ng = SB_HEADS // hp
    w = hp * HEAD_DIM
    return pl.pallas_call(
        _sb_kernel,
        grid=(batch, ng, nq),
        in_specs=[pl.BlockSpec((t, w), lambda b, h, i: (b * nq + i, h)),
                  pl.BlockSpec((seq, w), lambda b, h, i: (b, ng + h)),
                  pl.BlockSpec((seq, w), lambda b, h, i: (b, 2 * ng + h))],
        out_specs=pl.BlockSpec((t, w), lambda b, h, i: (b * nq + i, h)),
        out_shape=jax.ShapeDtypeStruct((m, SB_HEADS * HEAD_DIM), BF16),
        scratch_shapes=[pltpu.VMEM((hp * t // SB_SUB, SB_SUB, HEAD_DIM), F32),
                        pltpu.VMEM((hp * t // SB_SUB, SB_SUB, 1), F32)],
        compiler_params=_params(3),
        name="sb_attention",
    )(qkv, qkv, qkv)


def _sb_mixer(h, w_in, batch, seq):
    qkv = _mm_plain(h, w_in, 0, 3 * SB_HEADS * HEAD_DIM)
    return _sb_attention(qkv, batch, seq)


def _trunk(x, c, layers, final_norm):
    batch, seq, d = x.shape
    c_pad = jnp.pad(c, ((0, SUBLANES - batch), (0, 0)))
    x2 = x.reshape(batch * seq, d)
    for kind, ada_w, ada_b, g_mix, g_ffn, mix, w_ff_in, w_ff_out in layers:
        mod3 = _adaln(c_pad, ada_w, ada_b).reshape(SUBLANES * 6, 1, d)
        h = _norm_mod(x2, g_mix, mod3, 1, 0, seq)
        if kind == 0:
            o = _dsa_mixer(h, mix[0], batch, seq)
        elif kind == 1:
            o = _mla_mixer(h, *mix[:5], batch, seq)
        else:
            o = _sb_mixer(h, mix[0], batch, seq)
        x2 = _mm_resid(o, mix[-1], x2, mod3, 2, seq, tn=MIXER_OUT_TN)
        h = _norm_mod(x2, g_ffn, mod3, 4, 3, seq)
        d_ff = w_ff_out.shape[0]
        u = _mm_swiglu(h, w_ff_in, d_ff)
        x2 = _mm_resid(u, w_ff_out, x2, mod3, 5, seq)
    return _norm_final(x2, final_norm).reshape(batch, seq, d)


def kernel(x, c, ada_w_0, ada_b_0, norm_mix_0, norm_ffn_0, dsa_w_in_0, dsa_w_out_0, ffn_w_in_0, ffn_w_out_0, ada_w_1, ada_b_1, norm_mix_1, norm_ffn_1, mla_w_down_1, mla_q_norm_1, mla_kv_norm_1, mla_w_uq_1, mla_w_ukv_1, mla_w_out_1, ffn_w_in_1, ffn_w_out_1, ada_w_2, ada_b_2, norm_mix_2, norm_ffn_2, sb_w_in_2, sb_w_out_2, ffn_w_in_2, ffn_w_out_2, ada_w_3, ada_b_3, norm_mix_3, norm_ffn_3, dsa_w_in_3, dsa_w_out_3, ffn_w_in_3, ffn_w_out_3, final_norm):
    layers = (
        (0, ada_w_0, ada_b_0, norm_mix_0, norm_ffn_0, (dsa_w_in_0, dsa_w_out_0), ffn_w_in_0, ffn_w_out_0),
        (1, ada_w_1, ada_b_1, norm_mix_1, norm_ffn_1,
         (mla_w_down_1, mla_q_norm_1, mla_kv_norm_1, mla_w_uq_1, mla_w_ukv_1, mla_w_out_1),
         ffn_w_in_1, ffn_w_out_1),
        (2, ada_w_2, ada_b_2, norm_mix_2, norm_ffn_2, (sb_w_in_2, sb_w_out_2), ffn_w_in_2, ffn_w_out_2),
        (0, ada_w_3, ada_b_3, norm_mix_3, norm_ffn_3, (dsa_w_in_3, dsa_w_out_3), ffn_w_in_3, ffn_w_out_3),
    )
    return _trunk(x, c, layers, final_norm)
```
